```python
import jax, jax.numpy as jnp
from jax import lax
import numpy as np

D_MODEL = 2048
BATCH = 8
SEQ = 2048
DEPTH = 2

N_A_LAYERS = DEPTH // 2
N_B_LAYERS = DEPTH - N_A_LAYERS

GLA_HEADS = 4
GLA_DK = D_MODEL // 2 // GLA_HEADS
GLA_DV = D_MODEL // GLA_HEADS
GLA_GATE_RANK = 16
GLA_GATE_NORM = 16.0
GLA_CHUNK = 64
GLA_IN = 2 * GLA_HEADS * GLA_DK + 2 * GLA_HEADS * GLA_DV + GLA_GATE_RANK

MLA_HEADS = D_MODEL // 128
MLA_Q_RANK = D_MODEL // 4
MLA_KV_RANK = D_MODEL // 4
MLA_NOPE = 128
MLA_ROPE = 64
MLA_V = 128
ROPE_THETA = 10000.0
ATTN_BLOCK = 128

D_FF = 5632
CONV_W = 3

EPS = 1e-6

kernel_name = "yoco_gla_mla_convffn_adaln"


def rmsnorm(x, g):
    xf = x.astype(jnp.float32)
    y = xf * lax.rsqrt(jnp.mean(xf * xf, axis=-1, keepdims=True) + EPS)
    return (y * g.astype(jnp.float32)).astype(x.dtype)


def modulate(h, shift, scale):
    return h * (1 + scale[:, None, :]) + shift[:, None, :]


def rope_tables(positions):
    inv = ROPE_THETA ** (-jnp.arange(0, MLA_ROPE, 2, dtype=jnp.float32) / MLA_ROPE)
    ang = positions.astype(jnp.float32)[..., None] * inv
    return jnp.cos(ang), jnp.sin(ang)


def apply_rope(x, cos, sin):
    xf = x.astype(jnp.float32)
    x1, x2 = jnp.split(xf, 2, axis=-1)
    return jnp.concatenate([x1 * cos - x2 * sin, x2 * cos + x1 * sin], axis=-1).astype(x.dtype)


def gla_chunked(q, k, v, log_a):
    B_, S_, H_, DK_ = q.shape
    DV_ = v.shape[-1]
    nc = S_ // GLA_CHUNK

    def to_chunks(t):
        d = t.shape[-1]
        return t.astype(jnp.float32).reshape(B_, nc, GLA_CHUNK, H_, d).transpose(1, 0, 3, 2, 4)

    qc, kc, vc, gc = to_chunks(q), to_chunks(k), to_chunks(v), to_chunks(log_a)
    tri = jnp.tril(jnp.ones((GLA_CHUNK, GLA_CHUNK), dtype=bool))

    def step(state, inp):
        qb, kb, vb, gb = inp
        b = jnp.cumsum(gb, axis=2)
        inter = jnp.einsum('bhid,bhdv->bhiv', qb * jnp.exp(b), state)
        diff = b[:, :, :, None, :] - b[:, :, None, :, :]
        decay = jnp.exp(jnp.where(tri[:, :, None], diff, -jnp.inf))
        scores = jnp.einsum('bhid,bhjd,bhijd->bhij', qb, kb, decay)
        intra = jnp.einsum('bhij,bhjv->bhiv', scores, vb)
        b_last = b[:, :, -1:, :]
        state = (jnp.exp(b_last[:, :, 0, :])[..., None] * state
                 + jnp.einsum('bhjd,bhjv->bhdv', kb * jnp.exp(b_last - b), vb))
        return state, inter + intra

    state0 = jnp.zeros((B_, H_, DK_, DV_), jnp.float32)
    _, out = lax.scan(step, state0, (qc, kc, vc, gc))
    return out.transpose(1, 0, 3, 2, 4).reshape(B_, S_, H_, DV_)


def gla_mixer(h, w_in, w_gate2, b_gate, head_g, w_o):
    B_, S_, _ = h.shape
    hk, hv = GLA_HEADS * GLA_DK, GLA_HEADS * GLA_DV
    proj = h @ w_in
    q, k, v, r, g1 = jnp.split(proj, [hk, 2 * hk, 2 * hk + hv, 2 * hk + 2 * hv], axis=-1)
    log_a = jax.nn.log_sigmoid((g1 @ w_gate2 + b_gate).astype(jnp.float32)) / GLA_GATE_NORM
    shp_k = (B_, S_, GLA_HEADS, GLA_DK)
    o = gla_chunked((q * GLA_DK ** -0.5).reshape(shp_k), k.reshape(shp_k),
                    v.reshape(B_, S_, GLA_HEADS, GLA_DV), log_a.reshape(shp_k))
    o = rmsnorm(o, head_g).astype(h.dtype)
    o = o * jax.nn.silu(r).reshape(B_, S_, GLA_HEADS, GLA_DV)
    return o.reshape(B_, S_, hv) @ w_o


def mla_shared_kv(x, c_act, kv_norm_g, kv_mod_w, kv_mod_b, w_dkv, kv_latent_g, w_ukv, cos, sin):
    B_, S_, _ = x.shape
    shift, scale = jnp.split(c_act @ kv_mod_w + kv_mod_b, 2, axis=-1)
    h = modulate(rmsnorm(x, kv_norm_g), shift, scale)
    ckv = h @ w_dkv
    latent, k_rope = ckv[..., :MLA_KV_RANK], ckv[..., MLA_KV_RANK:]
    kv = (rmsnorm(latent, kv_latent_g) @ w_ukv).reshape(B_, S_, MLA_HEADS, MLA_NOPE + MLA_V)
    k_nope, v = kv[..., :MLA_NOPE], kv[..., MLA_NOPE:]
    k_rope = apply_rope(k_rope, cos, sin)
    return k_nope, k_rope, v


def mla_attention(h, w_dq, q_norm_g, w_uq, w_o, k_nope, k_rope, v, cos, sin):
    B_, S_, _ = h.shape
    q = (rmsnorm(h @ w_dq, q_norm_g) @ w_uq).reshape(B_, S_, MLA_HEADS, MLA_NOPE + MLA_ROPE)
    q_nope = q[..., :MLA_NOPE]
    q_rope = apply_rope(q[..., MLA_NOPE:], cos[:, :, None, :], sin[:, :, None, :])
    scale = (MLA_NOPE + MLA_ROPE) ** -0.5
    outs = []
    for blk in range(S_ // ATTN_BLOCK):
        q0, q1 = blk * ATTN_BLOCK, (blk + 1) * ATTN_BLOCK
        s = (jnp.einsum('bqhd,bkhd->bhqk', q_nope[:, q0:q1], k_nope[:, :q1])
             + jnp.einsum('bqhr,bkr->bhqk', q_rope[:, q0:q1], k_rope[:, :q1]))
        s = s.astype(jnp.float32) * scale
        mask = jnp.arange(q1)[None, :] <= jnp.arange(q0, q1)[:, None]
        p = jax.nn.softmax(jnp.where(mask, s, -jnp.inf), axis=-1).astype(v.dtype)
        outs.append(jnp.einsum('bhqk,bkhv->bqhv', p, v[:, :q1]))
    o = jnp.concatenate(outs, axis=1).reshape(B_, S_, MLA_HEADS * MLA_V)
    return o @ w_o


def conv_ffn(h, w_up, conv_w, conv_b, w_down):
    u = h @ w_up
    u = lax.conv_general_dilated(
        u, conv_w[:, None, :].astype(u.dtype), window_strides=(1,), padding=[(CONV_W - 1, 0)],
        dimension_numbers=('NWC', 'WIO', 'NWC'), feature_group_count=u.shape[-1]) + conv_b
    val, gate = jnp.split(u, 2, axis=-1)
    return (jax.nn.silu(gate) * val) @ w_down


def setup_inputs(seed: int = 0) -> dict:
    key = jax.random.key(seed)
    ks = iter(jax.random.split(key, 40))
    f32 = jnp.float32

    def nrm(shape, scale):
        return jax.random.normal(next(ks), shape, f32) * scale

    def gain(shape):
        return 1.0 + nrm(shape, 0.01)

    D = D_MODEL
    x = nrm((BATCH, SEQ, D), 1.0)
    c = nrm((BATCH, D), 1.0)
    offset = jax.random.randint(next(ks), (BATCH, 1), 0, 4096, dtype=jnp.int32)
    positions = offset + jnp.arange(SEQ, dtype=jnp.int32)[None, :]
    return {
        "x": x,
        "c": c,
        "positions": positions,
        "mod_w": nrm((DEPTH, D, 6 * D), 0.5 * D ** -0.5),
        "mod_b": nrm((DEPTH, 6 * D), 0.01),
        "norm_mix_g": gain((DEPTH, D)),
        "norm_ffn_g": gain((DEPTH, D)),
        "gla_w_in": nrm((N_A_LAYERS, D, GLA_IN), D ** -0.5),
        "gla_w_gate2": nrm((N_A_LAYERS, GLA_GATE_RANK, GLA_HEADS * GLA_DK), GLA_GATE_RANK ** -0.5),
        "gla_b_gate": nrm((N_A_LAYERS, GLA_HEADS * GLA_DK), 0.01),
        "gla_head_g": gain((N_A_LAYERS, GLA_HEADS, GLA_DV)),
        "gla_w_o": nrm((N_A_LAYERS, GLA_HEADS * GLA_DV, D), (GLA_HEADS * GLA_DV) ** -0.5),
        "kv_norm_g": gain((D,)),
        "kv_mod_w": nrm((D, 2 * D), 0.5 * D ** -0.5),
        "kv_mod_b": nrm((2 * D,), 0.01),
        "mla_w_dkv": nrm((D, MLA_KV_RANK + MLA_ROPE), D ** -0.5),
        "mla_kv_norm_g": gain((MLA_KV_RANK,)),
        "mla_w_ukv": nrm((MLA_KV_RANK, MLA_HEADS * (MLA_NOPE + MLA_V)), MLA_KV_RANK ** -0.5),
        "mla_w_dq": nrm((N_B_LAYERS, D, MLA_Q_RANK), D ** -0.5),
        "mla_q_norm_g": gain((N_B_LAYERS, MLA_Q_RANK)),
        "mla_w_uq": nrm((N_B_LAYERS, MLA_Q_RANK, MLA_HEADS * (MLA_NOPE + MLA_ROPE)), MLA_Q_RANK ** -0.5),
        "mla_w_o": nrm((N_B_LAYERS, MLA_HEADS * MLA_V, D), (MLA_HEADS * MLA_V) ** -0.5),
        "ffn_w_up": nrm((DEPTH, D, 2 * D_FF), D ** -0.5),
        "ffn_conv_w": nrm((DEPTH, CONV_W, 2 * D_FF), CONV_W ** -0.5),
        "ffn_conv_b": nrm((DEPTH, 2 * D_FF), 0.01),
        "ffn_w_down": nrm((DEPTH, D_FF, D), D_FF ** -0.5),
        "final_norm_g": gain((D,)),
    }


def reference(x, c, positions, mod_w, mod_b, norm_mix_g, norm_ffn_g,
              gla_w_in, gla_w_gate2, gla_b_gate, gla_head_g, gla_w_o,
              kv_norm_g, kv_mod_w, kv_mod_b, mla_w_dkv, mla_kv_norm_g, mla_w_ukv,
              mla_w_dq, mla_q_norm_g, mla_w_uq, mla_w_o,
              ffn_w_up, ffn_conv_w, ffn_conv_b, ffn_w_down, final_norm_g):
    c_act = jax.nn.silu(c)
    cos, sin = rope_tables(positions)
    shared_kv = None
    for i in range(DEPTH):
        mod = c_act @ mod_w[i] + mod_b[i]
        sh_m, sc_m, g_m, sh_f, sc_f, g_f = jnp.split(mod, 6, axis=-1)
        h = modulate(rmsnorm(x, norm_mix_g[i]), sh_m, sc_m)
        if i < N_A_LAYERS:
            j = i
            y = gla_mixer(h, gla_w_in[j], gla_w_gate2[j], gla_b_gate[j], gla_head_g[j], gla_w_o[j])
        else:
            if i == N_A_LAYERS:
                shared_kv = mla_shared_kv(x, c_act, kv_norm_g, kv_mod_w, kv_mod_b,
                                          mla_w_dkv, mla_kv_norm_g, mla_w_ukv, cos, sin)
            j = i - N_A_LAYERS
            k_nope, k_rope, v = shared_kv
            y = mla_attention(h, mla_w_dq[j], mla_q_norm_g[j], mla_w_uq[j], mla_w_o[j],
                              k_nope, k_rope, v, cos, sin)
        x = x + (1 + g_m)[:, None, :] * y
        h = modulate(rmsnorm(x, norm_ffn_g[i]), sh_f, sc_f)
        x = x + (1 + g_f)[:, None, :] * conv_ffn(h, ffn_w_up[i], ffn_conv_w[i], ffn_conv_b[i], ffn_w_down[i])
    return rmsnorm(x, final_norm_g)
```

```python
import functools

import jax
import jax.numpy as jnp
from jax import lax
from jax.experimental import pallas as pl
from jax.experimental.pallas import tpu as pltpu

F32 = jnp.float32
BF16 = jnp.bfloat16

EPS = 1e-6
LANE = 128
BF16_ROWS = 16
VMEM_LIMIT_BYTES = 56 << 20

GLA_HEADS = 4
GLA_DK = 256
GLA_DV = 512
GLA_GATE_RANK = 16
GLA_GATE_NORM = 16.0
GLA_CHUNK = 64
GLA_SUB = 16

MLA_HEADS = 16
MLA_NOPE = 128
MLA_ROPE = 64
MLA_V = 128
MLA_KV_RANK = 512
ROPE_THETA = 10000.0

CONV_W = 3


def _params(*sem):
    return pltpu.CompilerParams(dimension_semantics=sem, vmem_limit_bytes=VMEM_LIMIT_BYTES)


def _dot(a, b):
    return jnp.dot(a, b, preferred_element_type=F32)


def _dot_nt(a, b):
    return lax.dot_general(a, b, (((1,), (1,)), ((), ())), preferred_element_type=F32)


def _dot_tn(a, b):
    return lax.dot_general(a, b, (((0,), (0,)), ((), ())), preferred_element_type=F32)


def _sigmoid(x):
    return 1.0 / (1.0 + jnp.exp(-x))


def _rmsnorm(x, g):
    return x * lax.rsqrt(jnp.mean(x * x, axis=-1, keepdims=True) + EPS) * g


def _normmod(x, g, shift, scale):
    return _rmsnorm(x, g) * (1.0 + scale) + shift


def _mod_kernel(c_ref, w_ref, b_ref, o_ref):
    c = c_ref[...]
    c_act = (c * _sigmoid(c)).astype(BF16)
    o_ref[...] = _dot(c_act, w_ref[...].astype(BF16)) + b_ref[...]


def _mod_linear(c, w, b, *, tn=1024):
    L, D, N = w.shape
    B = c.shape[0]
    return pl.pallas_call(
        _mod_kernel,
        grid=(L, N // tn),
        in_specs=[
            pl.BlockSpec((B, D), lambda l, j: (0, 0)),
            pl.BlockSpec((None, D, tn), lambda l, j: (l, 0, j)),
            pl.BlockSpec((None, 1, tn), lambda l, j: (l, 0, j)),
        ],
        out_specs=pl.BlockSpec((None, B, tn), lambda l, j: (l, 0, j)),
        out_shape=jax.ShapeDtypeStruct((L, B, N), F32),
        compiler_params=_params("parallel", "parallel"),
    )(c, w, b)


def _normmod_matmul_kernel(x_ref, g_ref, sh_ref, sc_ref, w_ref, o_ref, hn_ref):
    @pl.when(pl.program_id(1) == 0)
    def _():
        hn_ref[...] = _normmod(x_ref[...], g_ref[...], sh_ref[...], sc_ref[...]).astype(BF16)

    o_ref[...] = _dot(hn_ref[...], w_ref[...]).astype(o_ref.dtype)


def _normmod_matmul(x, g, shift, scale, w, seq, *, tm, tn, out_dtype):
    T, D = x.shape
    N = w.shape[1]
    tm = min(tm, seq)
    per_seq = seq // tm
    return pl.pallas_call(
        _normmod_matmul_kernel,
        grid=(T // tm, N // tn),
        in_specs=[
            pl.BlockSpec((tm, D), lambda i, j: (i, 0)),
            pl.BlockSpec((1, D), lambda i, j: (0, 0)),
            pl.BlockSpec((None, 1, D), lambda i, j: (i // per_seq, 0, 0)),
            pl.BlockSpec((None, 1, D), lambda i, j: (i // per_seq, 0, 0)),
            pl.BlockSpec((D, tn), lambda i, j: (0, j)),
        ],
        out_specs=pl.BlockSpec((tm, tn), lambda i, j: (i, j)),
        out_shape=jax.ShapeDtypeStruct((T, N), out_dtype),
        scratch_shapes=[pltpu.VMEM((tm, D), BF16)],
        compiler_params=_params("parallel", "arbitrary"),
    )(x, g, shift, scale, w)


def _matmul_residual_kernel(a_ref, w_ref, res_ref, gate_ref, o_ref):
    o_ref[...] = res_ref[...] + (1.0 + gate_ref[...]) * _dot(a_ref[...], w_ref[...])


def _matmul_residual(a, w, res, gate, seq, *, tm=1024, tn=1024):
    T, K = a.shape
    N = w.shape[1]
    tm = min(tm, seq)
    per_seq = seq // tm
    return pl.pallas_call(
        _matmul_residual_kernel,
        grid=(T // tm, N // tn),
        in_specs=[
            pl.BlockSpec((tm, K), lambda i, j: (i, 0)),
            pl.BlockSpec((K, tn), lambda i, j: (0, j)),
            pl.BlockSpec((tm, tn), lambda i, j: (i, j)),
            pl.BlockSpec((None, 1, tn), lambda i, j: (i // per_seq, 0, j)),
        ],
        out_specs=pl.BlockSpec((tm, tn), lambda i, j: (i, j)),
        out_shape=jax.ShapeDtypeStruct((T, N), F32),
        compiler_params=_params("parallel", "parallel"),
    )(a, w, res, gate)


def _gla_chunk(q, k, v, g1, w2, b_gate, state_ref, b_scr, k_scr):
    C, DK = q.shape
    z = _dot(g1, w2) + b_gate
    log_a = (jnp.minimum(z, 0.0) - jnp.log1p(jnp.exp(-jnp.abs(z)))) * (1.0 / GLA_GATE_NORM)

    la_hi = log_a.astype(BF16)
    rem = log_a - la_hi.astype(F32)
    la_mid = rem.astype(BF16)
    la_lo = (rem - la_mid.astype(F32)).astype(BF16)
    row = lax.broadcasted_iota(jnp.int32, (C, C), 0)
    col = lax.broadcasted_iota(jnp.int32, (C, C), 1)
    tri = jnp.where(col <= row, 1.0, 0.0).astype(BF16)
    b = _dot(tri, la_hi) + _dot(tri, la_mid) + _dot(tri, la_lo)
    b_last = b[C - 1:C, :]

    state = state_ref[...]
    inter = _dot((q * jnp.exp(b)).astype(BF16), state.astype(BF16))

    n_sub = C // GLA_SUB
    off_rows = [jnp.zeros((GLA_SUB, C), F32)]
    for blk in range(1, n_sub):
        lo = blk * GLA_SUB
        b_ref_row = b[lo:lo + 1, :]
        q_blk = q[lo:lo + GLA_SUB] * jnp.exp(b[lo:lo + GLA_SUB] - b_ref_row)
        k_all = k * jnp.exp(jnp.minimum(b_ref_row - b, 0.0))
        off_rows.append(_dot_nt(q_blk.astype(BF16), k_all.astype(BF16)))
    s_off = jnp.concatenate(off_rows, axis=0)

    b_scr[...] = b
    k_scr[...] = k
    blk_start = (row // GLA_SUB) * GLA_SUB
    s_diag = jnp.zeros((C, C), F32)
    for jj in range(GLA_SUB):
        b_j = jnp.concatenate(
            [jnp.broadcast_to(b_scr[blk * GLA_SUB + jj:blk * GLA_SUB + jj + 1, :], (GLA_SUB, DK))
             for blk in range(n_sub)], axis=0)
        k_j = jnp.concatenate(
            [jnp.broadcast_to(k_scr[blk * GLA_SUB + jj:blk * GLA_SUB + jj + 1, :], (GLA_SUB, DK))
             for blk in range(n_sub)], axis=0)
        e = q * k_j * jnp.exp(jnp.minimum(b - b_j, 0.0))
        s_diag = jnp.where(col == blk_start + jj, jnp.sum(e, axis=-1, keepdims=True), s_diag)
    scores = jnp.where(col < blk_start, s_off, jnp.where(col <= row, s_diag, 0.0))
    intra = _dot(scores.astype(BF16), v)

    ones = jnp.ones((C, LANE), BF16)
    b_last_col = _dot_tn(la_hi, ones) + _dot_tn(la_mid, ones) + _dot_tn(la_lo, ones)
    decay_col = jnp.exp(b_last_col)
    decay = jnp.concatenate([decay_col] * (state.shape[1] // LANE), axis=1)
    k_dec = (k * jnp.exp(b_last - b)).astype(BF16)
    state_ref[...] = decay * state + _dot_tn(k_dec, v)
    return inter + intra


def _gla_kernel(q_ref, k_ref, v_ref, r_ref, g1_ref, w2_ref, bg_ref, hg_ref, o_ref,
                state_ref, b_scr, k_scr, *, rows):
    @pl.when(pl.program_id(2) == 0)
    def _():
        state_ref[...] = jnp.zeros_like(state_ref)

    w2 = w2_ref[...]
    b_gate = bg_ref[...]
    head_g = hg_ref[...]
    for c in range(rows // GLA_CHUNK):
        sl = pl.ds(c * GLA_CHUNK, GLA_CHUNK)
        q = q_ref[sl, :].astype(F32) * (GLA_DK ** -0.5)
        k = k_ref[sl, :].astype(F32)
        o = _gla_chunk(q, k, v_ref[sl, :], g1_ref[sl, :], w2, b_gate, state_ref, b_scr, k_scr)
        r = r_ref[sl, :].astype(F32)
        o_ref[sl, :] = (_rmsnorm(o, head_g) * (r * _sigmoid(r))).astype(o_ref.dtype)


def _gla(proj, w2, b_gate, head_g, batch, seq, *, rows=128):
    T = proj.shape[0]
    H, DK, DV = GLA_HEADS, GLA_DK, GLA_DV
    rows = min(rows, seq)
    nc = seq // rows
    rowmap = lambda b, h, c: b * nc + c
    v_off = 2 * H * DK // DV
    r_off = v_off + H
    g_off = (2 * H * DK + 2 * H * DV) // LANE
    return pl.pallas_call(
        functools.partial(_gla_kernel, rows=rows),
        grid=(batch, H, nc),
        in_specs=[
            pl.BlockSpec((rows, DK), lambda b, h, c: (rowmap(b, h, c), h)),
            pl.BlockSpec((rows, DK), lambda b, h, c: (rowmap(b, h, c), H + h)),
            pl.BlockSpec((rows, DV), lambda b, h, c: (rowmap(b, h, c), v_off + h)),
            pl.BlockSpec((rows, DV), lambda b, h, c: (rowmap(b, h, c), r_off + h)),
            pl.BlockSpec((rows, LANE), lambda b, h, c: (rowmap(b, h, c), g_off)),
            pl.BlockSpec((LANE, DK), lambda b, h, c: (0, h)),
            pl.BlockSpec((1, DK), lambda b, h, c: (0, h)),
            pl.BlockSpec((1, DV), lambda b, h, c: (0, h)),
        ],
        out_specs=pl.BlockSpec((rows, DV), lambda b, h, c: (rowmap(b, h, c), h)),
        out_shape=jax.ShapeDtypeStruct((T, H * DV), BF16),
        scratch_shapes=[
            pltpu.VMEM((DK, DV), F32),
            pltpu.VMEM((GLA_CHUNK, DK), F32),
            pltpu.VMEM((GLA_CHUNK, DK), F32),
        ],
        compiler_params=_params("parallel", "parallel", "arbitrary"),
    )(proj, proj, proj, proj, proj, w2, b_gate, head_g)


def _ffn_kernel(x_ref, xh_ref, g_ref, sh_ref, sc_ref, gate_ref, wv_ref, wg_ref, cwv_ref, cwg_ref,
                cbv_ref, cbg_ref, wd_ref, fg_ref, o_ref, hn_ref, acc_ref, *, tm, per_seq, final_norm):
    i = pl.program_id(0)
    j = pl.program_id(1)
    halo = BF16_ROWS

    @pl.when(j == 0)
    def _():
        g, sh, sc = g_ref[...], sh_ref[...], sc_ref[...]
        hn_ref[halo:, :] = _normmod(x_ref[...], g, sh, sc).astype(BF16)
        prev = _normmod(xh_ref[...], g, sh, sc)
        hn_ref[:halo, :] = jnp.where(i % per_seq == 0, 0.0, prev).astype(BF16)

    hn = hn_ref[...]

    def conv(u, cw_ref, cb_ref):
        cw = cw_ref[...]
        out = cb_ref[...] + cw[CONV_W - 1:CONV_W, :] * u[halo:halo + tm]
        for t in range(CONV_W - 1):
            shift = CONV_W - 1 - t
            out = out + cw[t:t + 1, :] * u[halo - shift:halo - shift + tm]
        return out

    val = conv(_dot(hn, wv_ref[...]), cwv_ref, cbv_ref)
    gt = conv(_dot(hn, wg_ref[...]), cwg_ref, cbg_ref)
    act = (gt * _sigmoid(gt) * val).astype(BF16)
    part = _dot(act, wd_ref[...])

    @pl.when(j == 0)
    def _():
        acc_ref[...] = part

    @pl.when(j > 0)
    def _():
        acc_ref[...] += part

    @pl.when(j == pl.num_programs(1) - 1)
    def _():
        y = x_ref[...] + (1.0 + gate_ref[...]) * acc_ref[...]
        if final_norm:
            y = _rmsnorm(y, fg_ref[...])
        o_ref[...] = y


def _conv_ffn(x, g, shift, scale, gate, w_up, conv_w, conv_b, w_down, final_g, seq, *,
              final_norm, tm=512, tf=512):
    T, D = x.shape
    F = w_down.shape[0]
    tm = min(tm, seq)
    per_seq = seq // tm
    nf = F // tf
    halo = BF16_ROWS
    hb = tm // halo
    batch_map = lambda i, j: (i // per_seq, 0, 0)
    return pl.pallas_call(
        functools.partial(_ffn_kernel, tm=tm, per_seq=per_seq, final_norm=final_norm),
        grid=(T // tm, nf),
        in_specs=[
            pl.BlockSpec((tm, D), lambda i, j: (i, 0)),
            pl.BlockSpec((halo, D), lambda i, j: (jnp.maximum(i * hb - 1, 0), 0)),
            pl.BlockSpec((1, D), lambda i, j: (0, 0)),
            pl.BlockSpec((None, 1, D), batch_map),
            pl.BlockSpec((None, 1, D), batch_map),
            pl.BlockSpec((None, 1, D), batch_map),
            pl.BlockSpec((D, tf), lambda i, j: (0, j)),
            pl.BlockSpec((D, tf), lambda i, j: (0, nf + j)),
            pl.BlockSpec((CONV_W, tf), lambda i, j: (0, j)),
            pl.BlockSpec((CONV_W, tf), lambda i, j: (0, nf + j)),
            pl.BlockSpec((1, tf), lambda i, j: (0, j)),
            pl.BlockSpec((1, tf), lambda i, j: (0, nf + j)),
            pl.BlockSpec((tf, D), lambda i, j: (j, 0)),
            pl.BlockSpec((1, D), lambda i, j: (0, 0)),
        ],
        out_specs=pl.BlockSpec((tm, D), lambda i, j: (i, 0)),
        out_shape=jax.ShapeDtypeStruct((T, D), F32),
        scratch_shapes=[pltpu.VMEM((tm + halo, D), BF16), pltpu.VMEM((tm, D), F32)],
        compiler_params=_params("parallel", "arbitrary"),
    )(x, x, g, shift, scale, gate, w_up, w_up, conv_w, conv_w, conv_b, conv_b, w_down, final_g)


def _mla_q_kernel(x_ref, g_ref, sh_ref, sc_ref, wdq_ref, qg_ref, wn_ref, wr_ref, wrs_ref,
                  cos_ref, sin_ref, qn_ref, qr_ref, *, scale):
    h = _normmod(x_ref[...], g_ref[...], sh_ref[...], sc_ref[...]).astype(BF16)
    cq = _rmsnorm(_dot(h, wdq_ref[...]), qg_ref[...]).astype(BF16)
    qn_ref[...] = (_dot(cq, wn_ref[...]) * scale).astype(BF16)
    cos = jnp.concatenate([cos_ref[...]] * MLA_HEADS, axis=1)
    sin = jnp.concatenate([sin_ref[...]] * MLA_HEADS, axis=1)
    qr = _dot(cq, wr_ref[...]) * cos + _dot(cq, wrs_ref[...]) * sin
    qr_ref[...] = (qr * scale).astype(BF16)


def _mla_q(x, g, shift, scale_mod, w_dq, q_g, w_n, w_r, w_rs, cos, sin, seq, *, tm=256):
    T, D = x.shape
    R = w_dq.shape[1]
    N = w_n.shape[1]
    tm = min(tm, seq)
    per_seq = seq // tm
    batch_map = lambda i: (i // per_seq, 0, 0)
    const = lambda i: (0, 0)
    rowblk = lambda i: (i, 0)
    return pl.pallas_call(
        functools.partial(_mla_q_kernel, scale=(MLA_NOPE + MLA_ROPE) ** -0.5),
        grid=(T // tm,),
        in_specs=[
            pl.BlockSpec((tm, D), rowblk),
            pl.BlockSpec((1, D), const),
            pl.BlockSpec((None, 1, D), batch_map),
            pl.BlockSpec((None, 1, D), batch_map),
            pl.BlockSpec((D, R), const),
            pl.BlockSpec((1, R), const),
            pl.BlockSpec((R, N), const),
            pl.BlockSpec((R, N), const),
            pl.BlockSpec((R, N), const),
            pl.BlockSpec((tm, LANE), rowblk),
            pl.BlockSpec((tm, LANE), rowblk),
        ],
        out_specs=[pl.BlockSpec((tm, N), rowblk), pl.BlockSpec((tm, N), rowblk)],
        out_shape=[jax.ShapeDtypeStruct((T, N), BF16), jax.ShapeDtypeStruct((T, N), BF16)],
        compiler_params=_params("parallel"),
    )(x, g, shift, scale_mod, w_dq, q_g, w_n, w_r, w_rs, cos, sin)


def _mla_kv_kernel(x_ref, g_ref, sh_ref, sc_ref, wdkv_ref, lg_ref, wk_ref, wv_ref,
                   cos_ref, sin_ref, kn_ref, kr_ref, v_ref):
    h = _normmod(x_ref[...], g_ref[...], sh_ref[...], sc_ref[...]).astype(BF16)
    ckv = _dot(h, wdkv_ref[...])
    lat = _rmsnorm(ckv[:, :MLA_KV_RANK], lg_ref[...]).astype(BF16)
    kn_ref[...] = _dot(lat, wk_ref[...]).astype(BF16)
    v_ref[...] = _dot(lat, wv_ref[...]).astype(BF16)
    kr = (ckv[:, MLA_KV_RANK:MLA_KV_RANK + LANE] * cos_ref[...]
          + ckv[:, MLA_KV_RANK + LANE:] * sin_ref[...])
    kr_ref[...] = kr.astype(BF16)


def _mla_kv(x, g, shift, scale_mod, w_dkv, lat_g, w_k, w_v, cos, sin, seq, *, tm=256):
    T, D = x.shape
    R = w_k.shape[0]
    N = w_k.shape[1]
    tm = min(tm, seq)
    per_seq = seq // tm
    batch_map = lambda i: (i // per_seq, 0, 0)
    const = lambda i: (0, 0)
    rowblk = lambda i: (i, 0)
    return pl.pallas_call(
        _mla_kv_kernel,
        grid=(T // tm,),
        in_specs=[
            pl.BlockSpec((tm, D), rowblk),
            pl.BlockSpec((1, D), const),
            pl.BlockSpec((None, 1, D), batch_map),
            pl.BlockSpec((None, 1, D), batch_map),
            pl.BlockSpec((D, w_dkv.shape[1]), const),
            pl.BlockSpec((1, R), const),
            pl.BlockSpec((R, N), const),
            pl.BlockSpec((R, N), const),
            pl.BlockSpec((tm, LANE), rowblk),
            pl.BlockSpec((tm, LANE), rowblk),
        ],
        out_specs=[pl.BlockSpec((tm, N), rowblk), pl.BlockSpec((tm, LANE), rowblk),
                   pl.BlockSpec((tm, N), rowblk)],
        out_shape=[jax.ShapeDtypeStruct((T, N), BF16), jax.ShapeDtypeStruct((T, LANE), BF16),
                   jax.ShapeDtypeStruct((T, N), BF16)],
        compiler_params=_params("parallel"),
    )(x, g, shift, scale_mod, w_dkv, lat_g, w_k, w_v, cos, sin)


def _attn_kernel(qn_ref, qr_ref, kn_ref, kr_ref, v_ref, o_ref, m_ref, l_ref, acc_ref, *, tq):
    qi = pl.program_id(2)
    q = jnp.concatenate([qn_ref[...], qr_ref[...]], axis=1)
    m_ref[...] = jnp.full_like(m_ref, -jnp.inf)
    l_ref[...] = jnp.zeros_like(l_ref)
    acc_ref[...] = jnp.zeros_like(acc_ref)

    def step(ki, masked):
        rows = pl.ds(pl.multiple_of(ki * tq, tq), tq)
        k = jnp.concatenate([kn_ref[rows, :], kr_ref[rows, :]], axis=1)
        s = _dot_nt(q, k)
        if masked:
            row = lax.broadcasted_iota(jnp.int32, s.shape, 0)
            col = lax.broadcasted_iota(jnp.int32, s.shape, 1)
            s = jnp.where(col <= row, s, -jnp.inf)
        m_prev = m_ref[...]
        m_new = jnp.maximum(m_prev, jnp.max(s, axis=-1, keepdims=True))
        alpha = jnp.exp(m_prev - m_new)
        p = jnp.exp(s - m_new)
        l_ref[...] = alpha * l_ref[...] + jnp.sum(p, axis=-1, keepdims=True)
        acc_ref[...] = alpha * acc_ref[...] + _dot(p.astype(BF16), v_ref[rows, :])
        m_ref[...] = m_new

    def body(ki, carry):
        step(ki, False)
        return carry

    lax.fori_loop(0, qi, body, 0)
    step(qi, True)
    o_ref[...] = (acc_ref[...] / l_ref[...]).astype(o_ref.dtype)


def _attention(qn, qr, kn, kr, v, batch, seq, *, tq=512):
    T = qn.shape[0]
    H = MLA_HEADS
    tq = min(tq, seq)
    nq = seq // tq
    qmap = lambda b, h, i: (b * nq + i, h)
    return pl.pallas_call(
        functools.partial(_attn_kernel, tq=tq),
        grid=(batch, H, nq),
        in_specs=[
            pl.BlockSpec((tq, LANE), qmap),
            pl.BlockSpec((tq, LANE), qmap),
            pl.BlockSpec((seq, LANE), lambda b, h, i: (b, h)),
            pl.BlockSpec((seq, LANE), lambda b, h, i: (b, 0)),
            pl.BlockSpec((seq, LANE), lambda b, h, i: (b, h)),
        ],
        out_specs=pl.BlockSpec((tq, LANE), qmap),
        out_shape=jax.ShapeDtypeStruct((T, H * MLA_V), BF16),
        scratch_shapes=[pltpu.VMEM((tq, 1), F32), pltpu.VMEM((tq, 1), F32),
                        pltpu.VMEM((tq, MLA_V), F32)],
        compiler_params=_params("parallel", "parallel", "arbitrary"),
    )(qn, qr, kn, kr, v)


def _pad_cols(w, n):
    return jnp.pad(w, [(0, 0)] * (w.ndim - 1) + [(0, n - w.shape[-1])])


def _rotate_half_cols(w):
    half = w.shape[-1] // 2
    return jnp.concatenate([-w[..., half:], w[..., :half]], axis=-1)


def kernel(x, c, positions, mod_w, mod_b, norm_mix_g, norm_ffn_g, gla_w_in, gla_w_gate2, gla_b_gate,
           gla_head_g, gla_w_o, kv_norm_g, kv_mod_w, kv_mod_b, mla_w_dkv, mla_kv_norm_g, mla_w_ukv,
           mla_w_dq, mla_q_norm_g, mla_w_uq, mla_w_o, ffn_w_up, ffn_conv_w, ffn_conv_b, ffn_w_down,
           final_norm_g):
    B, S, D = x.shape
    T = B * S
    depth = mod_w.shape[0]
    n_gla = gla_w_in.shape[0]
    H = MLA_HEADS

    mod = _mod_linear(c, mod_w, mod_b[:, None, :]).reshape(depth, B, 6, 1, D)
    kv_mod = _mod_linear(c, kv_mod_w[None], kv_mod_b[None, None, :]).reshape(B, 2, 1, D)

    inv = ROPE_THETA ** (-jnp.arange(0, MLA_ROPE, 2, dtype=F32) / MLA_ROPE)
    ang = positions.astype(F32)[..., None] * inv
    cos = jnp.tile(jnp.cos(ang), (1, 1, LANE // (MLA_ROPE // 2))).reshape(T, LANE)
    sin = jnp.tile(jnp.sin(ang), (1, 1, LANE // (MLA_ROPE // 2))).reshape(T, LANE)

    xs = x.reshape(T, D)
    for i in range(depth):
        sh_m, sc_m, g_m, sh_f, sc_f, g_f = (mod[i, :, t] for t in range(6))
        g_mix = norm_mix_g[i][None, :]
        if i < n_gla:
            gla_cols = 2 * GLA_HEADS * GLA_DK + 2 * GLA_HEADS * GLA_DV + LANE
            w_in = _pad_cols(gla_w_in[i], gla_cols).astype(BF16)
            proj = _normmod_matmul(xs, g_mix, sh_m, sc_m, w_in, S, tm=1024, tn=896, out_dtype=BF16)
            w2 = jnp.pad(gla_w_gate2[i], ((0, LANE - GLA_GATE_RANK), (0, 0))).astype(BF16)
            y = _gla(proj, w2, gla_b_gate[i][None, :], gla_head_g[i].reshape(1, -1), B, S)
            w_o = gla_w_o[i].astype(BF16)
        else:
            j = i - n_gla
            if j == 0:
                w_dkv = mla_w_dkv
                w_kr = w_dkv[:, MLA_KV_RANK:]
                w_dkv_cat = jnp.concatenate(
                    [w_dkv[:, :MLA_KV_RANK], _pad_cols(w_kr, LANE), _pad_cols(_rotate_half_cols(w_kr), LANE)],
                    axis=1).astype(BF16)
                w_ukv = mla_w_ukv.reshape(MLA_KV_RANK, H, MLA_NOPE + MLA_V)
                w_k = w_ukv[:, :, :MLA_NOPE].reshape(MLA_KV_RANK, H * MLA_NOPE).astype(BF16)
                w_v = w_ukv[:, :, MLA_NOPE:].reshape(MLA_KV_RANK, H * MLA_V).astype(BF16)
                kn, kr, v = _mla_kv(xs, kv_norm_g[None, :], kv_mod[:, 0], kv_mod[:, 1], w_dkv_cat,
                                    mla_kv_norm_g[None, :], w_k, w_v, cos, sin, S)
            w_uq = mla_w_uq[j].reshape(-1, H, MLA_NOPE + MLA_ROPE)
            rq = w_uq.shape[0]
            w_n = w_uq[:, :, :MLA_NOPE].reshape(rq, H * LANE).astype(BF16)
            w_rope = w_uq[:, :, MLA_NOPE:]
            w_r = _pad_cols(w_rope, LANE).reshape(rq, H * LANE).astype(BF16)
            w_rs = _pad_cols(_rotate_half_cols(w_rope), LANE).reshape(rq, H * LANE).astype(BF16)
            qn, qr = _mla_q(xs, g_mix, sh_m, sc_m, mla_w_dq[j].astype(BF16), mla_q_norm_g[j][None, :],
                            w_n, w_r, w_rs, cos, sin, S)
            y = _attention(qn, qr, kn, kr, v, B, S)
            w_o = mla_w_o[j].astype(BF16)
        xs = _matmul_residual(y, w_o, xs, g_m, S)
        xs = _conv_ffn(xs, norm_ffn_g[i][None, :], sh_f, sc_f, g_f, ffn_w_up[i].astype(BF16),
                       ffn_conv_w[i], ffn_conv_b[i][None, :], ffn_w_down[i].astype(BF16),
                       final_norm_g[None, :], S, final_norm=(i == depth - 1))
    return xs.reshape(B, S, D)
```

```python
import functools

import jax
import jax.numpy as jnp
from jax import lax
from jax.experimental import pallas as pl
from jax.experimental.pallas import tpu as pltpu

F32 = jnp.float32
BF16 = jnp.bfloat16

EPS = 1e-6
LANE = 128
BF16_ROWS = 16
VMEM_LIMIT_BYTES = 56 << 20

GLA_HEADS = 4
GLA_DK = 256
GLA_DV = 512
GLA_GATE_RANK = 16
GLA_GATE_NORM = 16.0
GLA_CHUNK = 64
GLA_SUB = 16

MLA_HEADS = 16
MLA_NOPE = 128
MLA_ROPE = 64
MLA_V = 128
MLA_KV_RANK = 512
ROPE_THETA = 10000.0

CONV_W = 3


def _params(*sem):
    return pltpu.CompilerParams(dimension_semantics=sem, vmem_limit_bytes=VMEM_LIMIT_BYTES)


def _dot(a, b):
    return jnp.dot(a, b, preferred_element_type=F32)


def _dot_nt(a, b):
    return lax.dot_general(a, b, (((1,), (1,)), ((), ())), preferred_element_type=F32)


def _dot_tn(a, b):
    return lax.dot_general(a, b, (((0,), (0,)), ((), ())), preferred_element_type=F32)


def _sigmoid(x):
    return 1.0 / (1.0 + jnp.exp(-x))


def _rmsnorm(x, g):
    return x * lax.rsqrt(jnp.mean(x * x, axis=-1, keepdims=True) + EPS) * g


def _normmod(x, g, shift, scale):
    return _rmsnorm(x, g) * (1.0 + scale) + shift


def _mod_kernel(c_ref, w_ref, b_ref, o_ref):
    c = c_ref[...]
    c_act = (c * _sigmoid(c)).astype(BF16)
    o_ref[...] = _dot(c_act, w_ref[...].astype(BF16)) + b_ref[...]


def _mod_linear(c, w, b, *, tn=1024):
    L, D, N = w.shape
    B = c.shape[0]
    return pl.pallas_call(
        _mod_kernel,
        grid=(L, N // tn),
        in_specs=[
            pl.BlockSpec((B, D), lambda l, j: (0, 0)),
            pl.BlockSpec((None, D, tn), lambda l, j: (l, 0, j)),
            pl.BlockSpec((None, 1, tn), lambda l, j: (l, 0, j)),
        ],
        out_specs=pl.BlockSpec((None, B, tn), lambda l, j: (l, 0, j)),
        out_shape=jax.ShapeDtypeStruct((L, B, N), F32),
        compiler_params=_params("parallel", "parallel"),
    )(c, w, b)


def _normmod_matmul_kernel(x_ref, g_ref, sh_ref, sc_ref, w_ref, o_ref, hn_ref):
    @pl.when(pl.program_id(1) == 0)
    def _():
        hn_ref[...] = _normmod(x_ref[...], g_ref[...], sh_ref[...], sc_ref[...]).astype(BF16)

    o_ref[...] = _dot(hn_ref[...], w_ref[...]).astype(o_ref.dtype)


def _normmod_matmul(x, g, shift, scale, w, seq, *, tm, tn, out_dtype):
    T, D = x.shape
    N = w.shape[1]
    tm = min(tm, seq)
    per_seq = seq // tm
    return pl.pallas_call(
        _normmod_matmul_kernel,
        grid=(T // tm, N // tn),
        in_specs=[
            pl.BlockSpec((tm, D), lambda i, j: (i, 0)),
            pl.BlockSpec((1, D), lambda i, j: (0, 0)),
            pl.BlockSpec((None, 1, D), lambda i, j: (i // per_seq, 0, 0)),
            pl.BlockSpec((None, 1, D), lambda i, j: (i // per_seq, 0, 0)),
            pl.BlockSpec((D, tn), lambda i, j: (0, j)),
        ],
        out_specs=pl.BlockSpec((tm, tn), lambda i, j: (i, j)),
        out_shape=jax.ShapeDtypeStruct((T, N), out_dtype),
        scratch_shapes=[pltpu.VMEM((tm, D), BF16)],
        compiler_params=_params("parallel", "arbitrary"),
    )(x, g, shift, scale, w)


def _matmul_residual_kernel(a_ref, w_ref, res_ref, gate_ref, o_ref):
    o_ref[...] = res_ref[...] + (1.0 + gate_ref[...]) * _dot(a_ref[...], w_ref[...])


def _matmul_residual(a, w, res, gate, seq, *, tm=1024, tn=1024):
    T, K = a.shape
    N = w.shape[1]
    tm = min(tm, seq)
    per_seq = seq // tm
    return pl.pallas_call(
        _matmul_residual_kernel,
        grid=(T // tm, N // tn),
        in_specs=[
            pl.BlockSpec((tm, K), lambda i, j: (i, 0)),
            pl.BlockSpec((K, tn), lambda i, j: (0, j)),
            pl.BlockSpec((tm, tn), lambda i, j: (i, j)),
            pl.BlockSpec((None, 1, tn), lambda i, j: (i // per_seq, 0, j)),
        ],
        out_specs=pl.BlockSpec((tm, tn), lambda i, j: (i, j)),
        out_shape=jax.ShapeDtypeStruct((T, N), F32),
        compiler_params=_params("parallel", "parallel"),
    )(a, w, res, gate)


def _gla_chunk(q, k, v, g1, w2, b_gate, state_ref, b_scr, k_scr):
    C, DK = q.shape
    z = _dot(g1, w2) + b_gate
    log_a = (jnp.minimum(z, 0.0) - jnp.log1p(jnp.exp(-jnp.abs(z)))) * (1.0 / GLA_GATE_NORM)

    la_hi = log_a.astype(BF16)
    rem = log_a - la_hi.astype(F32)
    la_mid = rem.astype(BF16)
    la_lo = (rem - la_mid.astype(F32)).astype(BF16)
    row = lax.broadcasted_iota(jnp.int32, (C, C), 0)
    col = lax.broadcasted_iota(jnp.int32, (C, C), 1)
    tri = jnp.where(col <= row, 1.0, 0.0).astype(BF16)
    b = _dot(tri, la_hi) + _dot(tri, la_mid) + _dot(tri, la_lo)
    b_last = b[C - 1:C, :]

    state = state_ref[...]
    inter = _dot((q * jnp.exp(b)).astype(BF16), state.astype(BF16))

    n_sub = C // GLA_SUB
    off_rows = [jnp.zeros((GLA_SUB, C), F32)]
    for blk in range(1, n_sub):
        lo = blk * GLA_SUB
        b_ref_row = b[lo:lo + 1, :]
        q_blk = q[lo:lo + GLA_SUB] * jnp.exp(b[lo:lo + GLA_SUB] - b_ref_row)
        k_all = k * jnp.exp(jnp.minimum(b_ref_row - b, 0.0))
        off_rows.append(_dot_nt(q_blk.astype(BF16), k_all.astype(BF16)))
    s_off = jnp.concatenate(off_rows, axis=0)

    b_scr[...] = b
    k_scr[...] = k
    blk_start = (row // GLA_SUB) * GLA_SUB
    s_diag = jnp.zeros((C, C), F32)
    for jj in range(GLA_SUB):
        b_j = jnp.concatenate(
            [jnp.broadcast_to(b_scr[blk * GLA_SUB + jj:blk * GLA_SUB + jj + 1, :], (GLA_SUB, DK))
             for blk in range(n_sub)], axis=0)
        k_j = jnp.concatenate(
            [jnp.broadcast_to(k_scr[blk * GLA_SUB + jj:blk * GLA_SUB + jj + 1, :], (GLA_SUB, DK))
             for blk in range(n_sub)], axis=0)
        e = q * k_j * jnp.exp(jnp.minimum(b - b_j, 0.0))
        s_diag = jnp.where(col == blk_start + jj, jnp.sum(e, axis=-1, keepdims=True), s_diag)
    scores = jnp.where(col < blk_start, s_off, jnp.where(col <= row, s_diag, 0.0))
    intra = _dot(scores.astype(BF16), v)

    ones = jnp.ones((C, LANE), BF16)
    b_last_col = _dot_tn(la_hi, ones) + _dot_tn(la_mid, ones) + _dot_tn(la_lo, ones)
    decay_col = jnp.exp(b_last_col)
    decay = jnp.concatenate([decay_col] * (state.shape[1] // LANE), axis=1)
    k_dec = (k * jnp.exp(b_last - b)).astype(BF16)
    state_ref[...] = decay * state + _dot_tn(k_dec, v)
    return inter + intra


def _gla_kernel(q_ref, k_ref, v_ref, r_ref, g1_ref, w2_ref, bg_ref, hg_ref, o_ref,
                state_ref, b_scr, k_scr, *, rows):
    @pl.when(pl.program_id(2) == 0)
    def _():
        state_ref[...] = jnp.zeros_like(state_ref)

    w2 = w2_ref[...]
    b_gate = bg_ref[...]
    head_g = hg_ref[...]
    for c in range(rows // GLA_CHUNK):
        sl = pl.ds(c * GLA_CHUNK, GLA_CHUNK)
        q = q_ref[sl, :].astype(F32) * (GLA_DK ** -0.5)
        k = k_ref[sl, :].astype(F32)
        o = _gla_chunk(q, k, v_ref[sl, :], g1_ref[sl, :], w2, b_gate, state_ref, b_scr, k_scr)
        r = r_ref[sl, :].astype(F32)
        o_ref[sl, :] = (_rmsnorm(o, head_g) * (r * _sigmoid(r))).astype(o_ref.dtype)


def _gla(proj, w2, b_gate, head_g, batch, seq, *, rows=128):
    T = proj.shape[0]
    H, DK, DV = GLA_HEADS, GLA_DK, GLA_DV
    rows = min(rows, seq)
    nc = seq // rows
    rowmap = lambda b, h, c: b * nc + c
    v_off = 2 * H * DK // DV
    r_off = v_off + H
    g_off = (2 * H * DK + 2 * H * DV) // LANE
    return pl.pallas_call(
        functools.partial(_gla_kernel, rows=rows),
        grid=(batch, H, nc),
        in_specs=[
            pl.BlockSpec((rows, DK), lambda b, h, c: (rowmap(b, h, c), h)),
            pl.BlockSpec((rows, DK), lambda b, h, c: (rowmap(b, h, c), H + h)),
            pl.BlockSpec((rows, DV), lambda b, h, c: (rowmap(b, h, c), v_off + h)),
            pl.BlockSpec((rows, DV), lambda b, h, c: (rowmap(b, h, c), r_off + h)),
            pl.BlockSpec((rows, LANE), lambda b, h, c: (rowmap(b, h, c), g_off)),
            pl.BlockSpec((LANE, DK), lambda b, h, c: (0, h)),
            pl.BlockSpec((1, DK), lambda b, h, c: (0, h)),
            pl.BlockSpec((1, DV), lambda b, h, c: (0, h)),
        ],
        out_specs=pl.BlockSpec((rows, DV), lambda b, h, c: (rowmap(b, h, c), h)),
        out_shape=jax.ShapeDtypeStruct((T, H * DV), BF16),
        scratch_shapes=[
            pltpu.VMEM((DK, DV), F32),
            pltpu.VMEM((GLA_CHUNK, DK), F32),
            pltpu.VMEM((GLA_CHUNK, DK), F32),
        ],
        compiler_params=_params("parallel", "parallel", "arbitrary"),
    )(proj, proj, proj, proj, proj, w2, b_gate, head_g)


def _ffn_kernel(x_ref, xh_ref, g_ref, sh_ref, sc_ref, gate_ref, wv_ref, wg_ref, cwv_ref, cwg_ref,
                cbv_ref, cbg_ref, wd_ref, fg_ref, o_ref, hn_ref, acc_ref, *, tm, per_seq, final_norm):
    i = pl.program_id(0)
    j = pl.program_id(1)
    halo = BF16_ROWS

    @pl.when(j == 0)
    def _():
        g, sh, sc = g_ref[...], sh_ref[...], sc_ref[...]
        hn_ref[halo:, :] = _normmod(x_ref[...], g, sh, sc).astype(BF16)
        prev = _normmod(xh_ref[...], g, sh, sc)
        hn_ref[:halo, :] = jnp.where(i % per_seq == 0, 0.0, prev).astype(BF16)
        acc_ref[...] = jnp.zeros_like(acc_ref)

    hn = hn_ref[...]

    def conv(u, cw_ref, cb_ref):
        cw = cw_ref[...]
        out = cb_ref[...] + cw[CONV_W - 1:CONV_W, :] * u[halo:halo + tm]
        for t in range(CONV_W - 1):
            shift = CONV_W - 1 - t
            out = out + cw[t:t + 1, :] * u[halo - shift:halo - shift + tm]
        return out

    val = conv(_dot(hn, wv_ref[...]), cwv_ref, cbv_ref)
    gt = conv(_dot(hn, wg_ref[...]), cwg_ref, cbg_ref)
    act = (gt * _sigmoid(gt) * val).astype(BF16)
    acc_ref[...] += _dot(act, wd_ref[...])

    @pl.when(j == pl.num_programs(1) - 1)
    def _():
        y = x_ref[...] + (1.0 + gate_ref[...]) * acc_ref[...]
        if final_norm:
            y = _rmsnorm(y, fg_ref[...])
        o_ref[...] = y


def _conv_ffn(x, g, shift, scale, gate, w_up, conv_w, conv_b, w_down, final_g, seq, *,
              final_norm, tm=512, tf=512):
    T, D = x.shape
    F = w_down.shape[0]
    tm = min(tm, seq)
    per_seq = seq // tm
    nf = F // tf
    halo = BF16_ROWS
    hb = tm // halo
    batch_map = lambda i, j: (i // per_seq, 0, 0)
    return pl.pallas_call(
        functools.partial(_ffn_kernel, tm=tm, per_seq=per_seq, final_norm=final_norm),
        grid=(T // tm, nf),
        in_specs=[
            pl.BlockSpec((tm, D), lambda i, j: (i, 0)),
            pl.BlockSpec((halo, D), lambda i, j: (jnp.maximum(i * hb - 1, 0), 0)),
            pl.BlockSpec((1, D), lambda i, j: (0, 0)),
            pl.BlockSpec((None, 1, D), batch_map),
            pl.BlockSpec((None, 1, D), batch_map),
            pl.BlockSpec((None, 1, D), batch_map),
            pl.BlockSpec((D, tf), lambda i, j: (0, j)),
            pl.BlockSpec((D, tf), lambda i, j: (0, nf + j)),
            pl.BlockSpec((CONV_W, tf), lambda i, j: (0, j)),
            pl.BlockSpec((CONV_W, tf), lambda i, j: (0, nf + j)),
            pl.BlockSpec((1, tf), lambda i, j: (0, j)),
            pl.BlockSpec((1, tf), lambda i, j: (0, nf + j)),
            pl.BlockSpec((tf, D), lambda i, j: (j, 0)),
            pl.BlockSpec((1, D), lambda i, j: (0, 0)),
        ],
        out_specs=pl.BlockSpec((tm, D), lambda i, j: (i, 0)),
        out_shape=jax.ShapeDtypeStruct((T, D), F32),
        scratch_shapes=[pltpu.VMEM((tm + halo, D), BF16), pltpu.VMEM((tm, D), F32)],
        compiler_params=_params("parallel", "arbitrary"),
    )(x, x, g, shift, scale, gate, w_up, w_up, conv_w, conv_w, conv_b, conv_b, w_down, final_g)


def _mla_q_kernel(x_ref, g_ref, sh_ref, sc_ref, wdq_ref, qg_ref, wn_ref, wr_ref, wrs_ref,
                  cos_ref, sin_ref, qn_ref, qr_ref, *, scale):
    h = _normmod(x_ref[...], g_ref[...], sh_ref[...], sc_ref[...]).astype(BF16)
    cq = _rmsnorm(_dot(h, wdq_ref[...]), qg_ref[...]).astype(BF16)
    qn_ref[...] = (_dot(cq, wn_ref[...]) * scale).astype(BF16)
    cos = jnp.concatenate([cos_ref[...]] * MLA_HEADS, axis=1)
    sin = jnp.concatenate([sin_ref[...]] * MLA_HEADS, axis=1)
    qr = _dot(cq, wr_ref[...]) * cos + _dot(cq, wrs_ref[...]) * sin
    qr_ref[...] = (qr * scale).astype(BF16)


def _mla_q(x, g, shift, scale_mod, w_dq, q_g, w_n, w_r, w_rs, cos, sin, seq, *, tm=256):
    T, D = x.shape
    R = w_dq.shape[1]
    N = w_n.shape[1]
    tm = min(tm, seq)
    per_seq = seq // tm
    batch_map = lambda i: (i // per_seq, 0, 0)
    const = lambda i: (0, 0)
    rowblk = lambda i: (i, 0)
    return pl.pallas_call(
        functools.partial(_mla_q_kernel, scale=(MLA_NOPE + MLA_ROPE) ** -0.5),
        grid=(T // tm,),
        in_specs=[
            pl.BlockSpec((tm, D), rowblk),
            pl.BlockSpec((1, D), const),
            pl.BlockSpec((None, 1, D), batch_map),
            pl.BlockSpec((None, 1, D), batch_map),
            pl.BlockSpec((D, R), const),
            pl.BlockSpec((1, R), const),
            pl.BlockSpec((R, N), const),
            pl.BlockSpec((R, N), const),
            pl.BlockSpec((R, N), const),
            pl.BlockSpec((tm, LANE), rowblk),
            pl.BlockSpec((tm, LANE), rowblk),
        ],
        out_specs=[pl.BlockSpec((tm, N), rowblk), pl.BlockSpec((tm, N), rowblk)],
        out_shape=[jax.ShapeDtypeStruct((T, N), BF16), jax.ShapeDtypeStruct((T, N), BF16)],
        compiler_params=_params("parallel"),
    )(x, g, shift, scale_mod, w_dq, q_g, w_n, w_r, w_rs, cos, sin)


def _mla_kv_kernel(x_ref, g_ref, sh_ref, sc_ref, wdkv_ref, lg_ref, wk_ref, wv_ref,
                   cos_ref, sin_ref, kn_ref, kr_ref, v_ref):
    h = _normmod(x_ref[...], g_ref[...], sh_ref[...], sc_ref[...]).astype(BF16)
    ckv = _dot(h, wdkv_ref[...])
    lat = _rmsnorm(ckv[:, :MLA_KV_RANK], lg_ref[...]).astype(BF16)
    kn_ref[...] = _dot(lat, wk_ref[...]).astype(BF16)
    v_ref[...] = _dot(lat, wv_ref[...]).astype(BF16)
    kr = (ckv[:, MLA_KV_RANK:MLA_KV_RANK + LANE] * cos_ref[...]
          + ckv[:, MLA_KV_RANK + LANE:] * sin_ref[...])
    kr_ref[...] = kr.astype(BF16)


def _mla_kv(x, g, shift, scale_mod, w_dkv, lat_g, w_k, w_v, cos, sin, seq, *, tm=256):
    T, D = x.shape
    R = w_k.shape[0]
    N = w_k.shape[1]
    tm = min(tm, seq)
    per_seq = seq // tm
    batch_map = lambda i: (i // per_seq, 0, 0)
    const = lambda i: (0, 0)
    rowblk = lambda i: (i, 0)
    return pl.pallas_call(
        _mla_kv_kernel,
        grid=(T // tm,),
        in_specs=[
            pl.BlockSpec((tm, D), rowblk),
            pl.BlockSpec((1, D), const),
            pl.BlockSpec((None, 1, D), batch_map),
            pl.BlockSpec((None, 1, D), batch_map),
            pl.BlockSpec((D, w_dkv.shape[1]), const),
            pl.BlockSpec((1, R), const),
            pl.BlockSpec((R, N), const),
            pl.BlockSpec((R, N), const),
            pl.BlockSpec((tm, LANE), rowblk),
            pl.BlockSpec((tm, LANE), rowblk),
        ],
        out_specs=[pl.BlockSpec((tm, N), rowblk), pl.BlockSpec((tm, LANE), rowblk),
                   pl.BlockSpec((tm, N), rowblk)],
        out_shape=[jax.ShapeDtypeStruct((T, N), BF16), jax.ShapeDtypeStruct((T, LANE), BF16),
                   jax.ShapeDtypeStruct((T, N), BF16)],
        compiler_params=_params("parallel"),
    )(x, g, shift, scale_mod, w_dkv, lat_g, w_k, w_v, cos, sin)


def _attn_kernel(qn_ref, qr_ref, kn_ref, kr_ref, v_ref, o_ref, *, blk):
    seq = qn_ref.shape[0]
    ones = jnp.ones((blk, LANE), BF16)
    row = lax.broadcasted_iota(jnp.int32, (blk, blk), 0)
    col = lax.broadcasted_iota(jnp.int32, (blk, blk), 1)
    for qi in range(seq // blk):
        qrows = pl.ds(qi * blk, blk)
        q = jnp.concatenate([qn_ref[qrows, :], qr_ref[qrows, :]], axis=1)
        m = jnp.full((blk, LANE), -jnp.inf, F32)
        acc = jnp.zeros((blk, 2 * LANE), F32)
        for ki in range(qi + 1):
            krows = pl.ds(ki * blk, blk)
            k = jnp.concatenate([kn_ref[krows, :], kr_ref[krows, :]], axis=1)
            s = _dot_nt(q, k)
            if ki == qi:
                s = jnp.where(col <= row, s, -jnp.inf)
            m_new = jnp.maximum(m, jnp.max(s, axis=-1, keepdims=True))
            alpha = jnp.exp(m - m_new)
            p = jnp.exp(s - jnp.concatenate([m_new] * (blk // LANE), axis=1))
            v1 = jnp.concatenate([v_ref[krows, :], ones], axis=1)
            acc = jnp.concatenate([alpha, alpha], axis=1) * acc + _dot(p.astype(BF16), v1)
            m = m_new
        o_ref[qrows, :] = (acc[:, :LANE] / acc[:, LANE:]).astype(o_ref.dtype)


def _attention(qn, qr, kn, kr, v, batch, seq, *, blk=512):
    T = qn.shape[0]
    H = MLA_HEADS
    blk = min(blk, seq)
    head = lambda b, h: (b, h)
    return pl.pallas_call(
        functools.partial(_attn_kernel, blk=blk),
        grid=(batch, H),
        in_specs=[
            pl.BlockSpec((seq, LANE), head),
            pl.BlockSpec((seq, LANE), head),
            pl.BlockSpec((seq, LANE), head),
            pl.BlockSpec((seq, LANE), lambda b, h: (b, 0)),
            pl.BlockSpec((seq, LANE), head),
        ],
        out_specs=pl.BlockSpec((seq, LANE), head),
        out_shape=jax.ShapeDtypeStruct((T, H * MLA_V), BF16),
        compiler_params=_params("parallel", "parallel"),
    )(qn, qr, kn, kr, v)


def _pad_cols(w, n):
    return jnp.pad(w, [(0, 0)] * (w.ndim - 1) + [(0, n - w.shape[-1])])


def _rotate_half_cols(w):
    half = w.shape[-1] // 2
    return jnp.concatenate([-w[..., half:], w[..., :half]], axis=-1)


def kernel(x, c, positions, mod_w, mod_b, norm_mix_g, norm_ffn_g, gla_w_in, gla_w_gate2, gla_b_gate,
           gla_head_g, gla_w_o, kv_norm_g, kv_mod_w, kv_mod_b, mla_w_dkv, mla_kv_norm_g, mla_w_ukv,
           mla_w_dq, mla_q_norm_g, mla_w_uq, mla_w_o, ffn_w_up, ffn_conv_w, ffn_conv_b, ffn_w_down,
           final_norm_g):
    B, S, D = x.shape
    T = B * S
    depth = mod_w.shape[0]
    n_gla = gla_w_in.shape[0]
    H = MLA_HEADS

    mod = _mod_linear(c, mod_w, mod_b[:, None, :]).reshape(depth, B, 6, 1, D)
    kv_mod = _mod_linear(c, kv_mod_w[None], kv_mod_b[None, None, :]).reshape(B, 2, 1, D)

    inv = ROPE_THETA ** (-jnp.arange(0, MLA_ROPE, 2, dtype=F32) / MLA_ROPE)
    ang = positions.astype(F32)[..., None] * inv
    cos = jnp.tile(jnp.cos(ang), (1, 1, LANE // (MLA_ROPE // 2))).reshape(T, LANE)
    sin = jnp.tile(jnp.sin(ang), (1, 1, LANE // (MLA_ROPE // 2))).reshape(T, LANE)

    xs = x.reshape(T, D)
    for i in range(depth):
        sh_m, sc_m, g_m, sh_f, sc_f, g_f = (mod[i, :, t] for t in range(6))
        g_mix = norm_mix_g[i][None, :]
        if i < n_gla:
            gla_cols = 2 * GLA_HEADS * GLA_DK + 2 * GLA_HEADS * GLA_DV + LANE
            w_in = _pad_cols(gla_w_in[i], gla_cols).astype(BF16)
            proj = _normmod_matmul(xs, g_mix, sh_m, sc_m, w_in, S, tm=1024, tn=896, out_dtype=BF16)
            w2 = jnp.pad(gla_w_gate2[i], ((0, LANE - GLA_GATE_RANK), (0, 0))).astype(BF16)
            y = _gla(proj, w2, gla_b_gate[i][None, :], gla_head_g[i].reshape(1, -1), B, S)
            w_o = gla_w_o[i].astype(BF16)
        else:
            j = i - n_gla
            if j == 0:
                w_dkv = mla_w_dkv
                w_kr = w_dkv[:, MLA_KV_RANK:]
                w_dkv_cat = jnp.concatenate(
                    [w_dkv[:, :MLA_KV_RANK], _pad_cols(w_kr, LANE), _pad_cols(_rotate_half_cols(w_kr), LANE)],
                    axis=1).astype(BF16)
                w_ukv = mla_w_ukv.reshape(MLA_KV_RANK, H, MLA_NOPE + MLA_V)
                w_k = w_ukv[:, :, :MLA_NOPE].reshape(MLA_KV_RANK, H * MLA_NOPE).astype(BF16)
                w_v = w_ukv[:, :, MLA_NOPE:].reshape(MLA_KV_RANK, H * MLA_V).astype(BF16)
                kn, kr, v = _mla_kv(xs, kv_norm_g[None, :], kv_mod[:, 0], kv_mod[:, 1], w_dkv_cat,
                                    mla_kv_norm_g[None, :], w_k, w_v, cos, sin, S)
            w_uq = mla_w_uq[j].reshape(-1, H, MLA_NOPE + MLA_ROPE)
            rq = w_uq.shape[0]
            w_n = w_uq[:, :, :MLA_NOPE].reshape(rq, H * LANE).astype(BF16)
            w_rope = w_uq[:, :, MLA_NOPE:]
            w_r = _pad_cols(w_rope, LANE).reshape(rq, H * LANE).astype(BF16)
            w_rs = _pad_cols(_rotate_half_cols(w_rope), LANE).reshape(rq, H * LANE).astype(BF16)
            qn, qr = _mla_q(xs, g_mix, sh_m, sc_m, mla_w_dq[j].astype(BF16), mla_q_norm_g[j][None, :],
                            w_n, w_r, w_rs, cos, sin, S)
            y = _attention(qn, qr, kn, kr, v, B, S)
            w_o = mla_w_o[j].astype(BF16)
        xs = _matmul_residual(y, w_o, xs, g_m, S)
        xs = _conv_ffn(xs, norm_ffn_g[i][None, :], sh_f, sc_f, g_f, ffn_w_up[i].astype(BF16),
                       ffn_conv_w[i], ffn_conv_b[i][None, :], ffn_w_down[i].astype(BF16),
                       final_norm_g[None, :], S, final_norm=(i == depth - 1))
    return xs.reshape(B, S, D)
```

```python
import functools

import jax
import jax.numpy as jnp
import numpy as np
from jax import lax
from jax.experimental import pallas as pl
from jax.experimental.pallas import tpu as pltpu

F32 = jnp.float32
BF16 = jnp.bfloat16

EPS = 1e-6
LANE = 128
SUBLANES = 8
BF16_ROWS = 16
VMEM_LIMIT_BYTES = 56 << 20

GLA_HEADS = 4
GLA_DK = 256
GLA_DV = 512
GLA_GATE_RANK = 16
GLA_GATE_NORM = 16.0
GLA_CHUNK = 64
GLA_LEVELS = 6
GLA_MIN_BCAST = 4

MLA_HEADS = 16
MLA_NOPE = 128
MLA_ROPE = 64
MLA_V = 128
MLA_KV_RANK = 512
ROPE_THETA = 10000.0

CONV_W = 3


def _params(*sem):
    return pltpu.CompilerParams(dimension_semantics=sem, vmem_limit_bytes=VMEM_LIMIT_BYTES)


def _dot(a, b):
    return jnp.dot(a, b, preferred_element_type=F32)


def _dot_nt(a, b):
    return lax.dot_general(a, b, (((1,), (1,)), ((), ())), preferred_element_type=F32)


def _dot_tn(a, b):
    return lax.dot_general(a, b, (((0,), (0,)), ((), ())), preferred_element_type=F32)


def _sigmoid(x):
    return 1.0 / (1.0 + jnp.exp(-x))


def _rmsnorm(x, g):
    return x * lax.rsqrt(jnp.mean(x * x, axis=-1, keepdims=True) + EPS) * g


def _normmod(x, g, shift, scale):
    return _rmsnorm(x, g) * (1.0 + scale) + shift


def _mod_kernel(c_ref, w_ref, b_ref, o_ref):
    c = c_ref[...]
    c_act = (c * _sigmoid(c)).astype(BF16)
    o_ref[...] = _dot(c_act, w_ref[...].astype(BF16)) + b_ref[...]


def _mod_linear(c, w, b, *, tn=1024):
    L, D, N = w.shape
    B = c.shape[0]
    return pl.pallas_call(
        _mod_kernel,
        grid=(L, N // tn),
        in_specs=[
            pl.BlockSpec((B, D), lambda l, j: (0, 0)),
            pl.BlockSpec((None, D, tn), lambda l, j: (l, 0, j)),
            pl.BlockSpec((None, 1, tn), lambda l, j: (l, 0, j)),
        ],
        out_specs=pl.BlockSpec((None, B, tn), lambda l, j: (l, 0, j)),
        out_shape=jax.ShapeDtypeStruct((L, B, N), F32),
        compiler_params=_params("parallel", "parallel"),
    )(c, w, b)


def _normmod_matmul_kernel(x_ref, g_ref, sh_ref, sc_ref, w_ref, o_ref, hn_ref):
    @pl.when(pl.program_id(1) == 0)
    def _():
        hn_ref[...] = _normmod(x_ref[...], g_ref[...], sh_ref[...], sc_ref[...]).astype(BF16)

    o_ref[...] = _dot(hn_ref[...], w_ref[...]).astype(o_ref.dtype)


def _normmod_matmul(x, g, shift, scale, w, seq, *, tm, tn, out_dtype):
    T, D = x.shape
    N = w.shape[1]
    tm = min(tm, seq)
    per_seq = seq // tm
    return pl.pallas_call(
        _normmod_matmul_kernel,
        grid=(T // tm, N // tn),
        in_specs=[
            pl.BlockSpec((tm, D), lambda i, j: (i, 0)),
            pl.BlockSpec((1, D), lambda i, j: (0, 0)),
            pl.BlockSpec((None, 1, D), lambda i, j: (i // per_seq, 0, 0)),
            pl.BlockSpec((None, 1, D), lambda i, j: (i // per_seq, 0, 0)),
            pl.BlockSpec((D, tn), lambda i, j: (0, j)),
        ],
        out_specs=pl.BlockSpec((tm, tn), lambda i, j: (i, j)),
        out_shape=jax.ShapeDtypeStruct((T, N), out_dtype),
        scratch_shapes=[pltpu.VMEM((tm, D), BF16)],
        compiler_params=_params("parallel", "arbitrary"),
    )(x, g, shift, scale, w)


def _matmul_residual_kernel(a_ref, w_ref, res_ref, gate_ref, o_ref):
    o_ref[...] = res_ref[...] + (1.0 + gate_ref[...]) * _dot(a_ref[...], w_ref[...])


def _matmul_residual(a, w, res, gate, seq, *, tm=1024, tn=1024):
    T, K = a.shape
    N = w.shape[1]
    tm = min(tm, seq)
    per_seq = seq // tm
    return pl.pallas_call(
        _matmul_residual_kernel,
        grid=(T // tm, N // tn),
        in_specs=[
            pl.BlockSpec((tm, K), lambda i, j: (i, 0)),
            pl.BlockSpec((K, tn), lambda i, j: (0, j)),
            pl.BlockSpec((tm, tn), lambda i, j: (i, j)),
            pl.BlockSpec((None, 1, tn), lambda i, j: (i // per_seq, 0, j)),
        ],
        out_specs=pl.BlockSpec((tm, tn), lambda i, j: (i, j)),
        out_shape=jax.ShapeDtypeStruct((T, N), F32),
        compiler_params=_params("parallel", "parallel"),
    )(a, w, res, gate)


def _gla_tables():
    C = GLA_CHUNK
    r = np.arange(C)
    tri = (r[None, :] <= r[:, None]).astype(np.float32)
    mats, sgn = [tri], []
    level = np.full((C, C), -1, np.int32)
    level[r, r] = GLA_LEVELS
    for l in range(GLA_LEVELS):
        h = C >> (l + 1)
        if h < GLA_MIN_BCAST:
            mats.append(tri[(r // (2 * h)) * (2 * h) + h])
        is_q = (r // h) % 2 == 1
        sgn.append(np.where(is_q, 1.0, -1.0))
        level[is_q[:, None] & ((r[None, :] // h) == (r[:, None] // h) - 1)] = l
    mstack = jnp.asarray(np.concatenate(mats, axis=0), BF16)
    sgn = jnp.asarray(np.repeat(np.concatenate(sgn)[:, None], GLA_DK, axis=1), F32)
    return mstack, sgn, jnp.asarray(level)


def _gla_decays(g1, w2, b_gate, mstack):
    z = _dot(g1, w2) + b_gate
    log_a = (jnp.minimum(z, 0.0) - jnp.log(1.0 + jnp.exp(-jnp.abs(z)))) * (1.0 / GLA_GATE_NORM)
    la_hi = log_a.astype(BF16)
    la_lo = (log_a - la_hi.astype(F32)).astype(BF16)
    return _dot(mstack, la_hi) + _dot(mstack, la_lo)


def _gla_local(q, k, v, bs, sgn_ref, level):
    C, DK = q.shape
    b = bs[:C]
    b_last = b[C - 1:C, :]
    n_fine = bs.shape[0] // C - 1
    scores = jnp.where(level == GLA_LEVELS, jnp.sum(q * k, axis=-1, keepdims=True), 0.0)
    for l in range(GLA_LEVELS):
        sg = sgn_ref[l * C:(l + 1) * C, :]
        h = C >> (l + 1)
        if h >= GLA_MIN_BCAST:
            b_ref = jnp.concatenate(
                [jnp.broadcast_to(b[r0 + h:r0 + h + 1, :], (2 * h, DK)) for r0 in range(0, C, 2 * h)], axis=0)
        else:
            fine = l - (GLA_LEVELS - n_fine)
            b_ref = bs[(fine + 1) * C:(fine + 2) * C]
        t = (jnp.where(sg > 0.0, q, k) * jnp.exp((b - b_ref) * sg)).astype(BF16)
        scores = jnp.where(level == l, _dot_nt(t, t), scores)
    intra = _dot(scores.astype(BF16), v)
    q_dec = (q * jnp.exp(b)).astype(BF16)
    k_dec = (k * jnp.exp(b_last - b)).astype(BF16)
    update = _dot_tn(k_dec, v)
    decay_col = jnp.exp(jnp.transpose(jnp.broadcast_to(b_last, (LANE, DK))))
    return intra, q_dec, update, decay_col


def _gla_kernel(q_ref, k_ref, v_ref, r_ref, g1_ref, w2_ref, bg_ref, hg_ref, ms_ref, sgn_ref, lv_ref,
                o_ref, state_ref, *, rows):
    @pl.when(pl.program_id(2) == 0)
    def _():
        state_ref[...] = jnp.zeros_like(state_ref)

    w2 = w2_ref[...]
    b_gate = bg_ref[...]
    head_g = hg_ref[...]
    mstack = ms_ref[...]
    level = lv_ref[...]
    chunks = [pl.ds(c * GLA_CHUNK, GLA_CHUNK) for c in range(rows // GLA_CHUNK)]
    decays = [_gla_decays(g1_ref[sl, :], w2, b_gate, mstack) for sl in chunks]
    local = []
    for sl, bs in zip(chunks, decays):
        q = q_ref[sl, :].astype(F32) * (GLA_DK ** -0.5)
        local.append(_gla_local(q, k_ref[sl, :].astype(F32), v_ref[sl, :], bs, sgn_ref, level))
    state = state_ref[...]
    for sl, (intra, q_dec, update, decay_col) in zip(chunks, local):
        o = _dot(q_dec, state.astype(BF16)) + intra
        state = jnp.concatenate([decay_col] * (state.shape[1] // LANE), axis=1) * state + update
        r = r_ref[sl, :].astype(F32)
        o_ref[sl, :] = (_rmsnorm(o, head_g) * (r * _sigmoid(r))).astype(o_ref.dtype)
    state_ref[...] = state


def _gla(proj, w2, b_gate, head_g, batch, seq, *, rows=512):
    T = proj.shape[0]
    H, DK, DV = GLA_HEADS, GLA_DK, GLA_DV
    rows = min(rows, seq)
    nc = seq // rows
    rowmap = lambda b, h, c: b * nc + c
    const = lambda b, h, c: (0, 0)
    v_off = 2 * H * DK // DV
    r_off = v_off + H
    g_off = (2 * H * DK + 2 * H * DV) // LANE
    mstack, sgn, level = _gla_tables()
    return pl.pallas_call(
        functools.partial(_gla_kernel, rows=rows),
        grid=(batch, H, nc),
        in_specs=[
            pl.BlockSpec((rows, DK), lambda b, h, c: (rowmap(b, h, c), h)),
            pl.BlockSpec((rows, DK), lambda b, h, c: (rowmap(b, h, c), H + h)),
            pl.BlockSpec((rows, DV), lambda b, h, c: (rowmap(b, h, c), v_off + h)),
            pl.BlockSpec((rows, DV), lambda b, h, c: (rowmap(b, h, c), r_off + h)),
            pl.BlockSpec((rows, LANE), lambda b, h, c: (rowmap(b, h, c), g_off)),
            pl.BlockSpec((LANE, DK), lambda b, h, c: (0, h)),
            pl.BlockSpec((1, DK), lambda b, h, c: (0, h)),
            pl.BlockSpec((1, DV), lambda b, h, c: (0, h)),
            pl.BlockSpec(mstack.shape, const),
            pl.BlockSpec(sgn.shape, const),
            pl.BlockSpec(level.shape, const),
        ],
        out_specs=pl.BlockSpec((rows, DV), lambda b, h, c: (rowmap(b, h, c), h)),
        out_shape=jax.ShapeDtypeStruct((T, H * DV), BF16),
        scratch_shapes=[pltpu.VMEM((DK, DV), F32)],
        compiler_params=_params("parallel", "parallel", "arbitrary"),
    )(proj, proj, proj, proj, proj, w2, b_gate, head_g, mstack, sgn, level)


def _ffn_kernel(x_ref, xh_ref, g_ref, sh_ref, sc_ref, gate_ref, wv_ref, wg_ref, cwv_ref, cwg_ref,
                cbv_ref, cbg_ref, wd_ref, fg_ref, o_ref, hn_ref, acc_ref, *, tm, per_seq, final_norm):
    i = pl.program_id(0)
    j = pl.program_id(1)
    halo = BF16_ROWS
    sub = SUBLANES
    span = tm // sub
    d_model = x_ref.shape[1]

    @pl.when(j == 0)
    def _():
        g, sh, sc = g_ref[...], sh_ref[...], sc_ref[...]
        h = _normmod(x_ref[...], g, sh, sc)
        h = pltpu.einshape("svd->vsd", h.reshape(sub, span, d_model)).reshape(tm, d_model)
        hn_ref[halo:, :] = h.astype(BF16)
        prev = _normmod(xh_ref[...], g, sh, sc)
        hn_ref[:halo, :] = jnp.where(i % per_seq == 0, 0.0, prev).astype(BF16)
        acc_ref[...] = jnp.zeros_like(acc_ref)

    hn = hn_ref[...]
    sub_id = lax.broadcasted_iota(jnp.int32, (sub, wv_ref.shape[1]), 0)

    def conv(u, cw_ref, cb_ref):
        cw = cw_ref[...]
        um = u[halo:]
        tail = u[halo - sub:halo]
        first1 = pltpu.roll(jnp.where(sub_id == sub - 1, tail, um[tm - sub:]), 1, 0)
        first2 = pltpu.roll(
            jnp.where(sub_id == sub - 1, pltpu.roll(tail, 1, 0), um[tm - 2 * sub:tm - sub]), 1, 0)
        back1 = jnp.concatenate([first1, um[:tm - sub]], axis=0)
        back2 = jnp.concatenate([first2, first1, um[:tm - 2 * sub]], axis=0)
        return cb_ref[...] + cw[2:3, :] * um + cw[1:2, :] * back1 + cw[0:1, :] * back2

    val = conv(_dot(hn, wv_ref[...]), cwv_ref, cbv_ref)
    gt = conv(_dot(hn, wg_ref[...]), cwg_ref, cbg_ref)
    act = (gt * _sigmoid(gt) * val).astype(BF16)
    acc_ref[...] += _dot(act, wd_ref[...])

    @pl.when(j == pl.num_programs(1) - 1)
    def _():
        acc = pltpu.einshape("vsd->svd", acc_ref[...].reshape(span, sub, d_model)).reshape(tm, d_model)
        y = x_ref[...] + (1.0 + gate_ref[...]) * acc
        if final_norm:
            y = _rmsnorm(y, fg_ref[...])
        o_ref[...] = y


def _conv_ffn(x, g, shift, scale, gate, w_up, conv_w, conv_b, w_down, final_g, seq, *,
              final_norm, tm=512, tf=512):
    assert conv_w.shape[0] == CONV_W == 3
    T, D = x.shape
    F = w_down.shape[0]
    tm = min(tm, seq)
    per_seq = seq // tm
    nf = F // tf
    halo = BF16_ROWS
    hb = tm // halo
    batch_map = lambda i, j: (i // per_seq, 0, 0)
    return pl.pallas_call(
        functools.partial(_ffn_kernel, tm=tm, per_seq=per_seq, final_norm=final_norm),
        grid=(T // tm, nf),
        in_specs=[
            pl.BlockSpec((tm, D), lambda i, j: (i, 0)),
            pl.BlockSpec((halo, D), lambda i, j: (jnp.maximum(i * hb - 1, 0), 0)),
            pl.BlockSpec((1, D), lambda i, j: (0, 0)),
            pl.BlockSpec((None, 1, D), batch_map),
            pl.BlockSpec((None, 1, D), batch_map),
            pl.BlockSpec((None, 1, D), batch_map),
            pl.BlockSpec((D, tf), lambda i, j: (0, j)),
            pl.BlockSpec((D, tf), lambda i, j: (0, nf + j)),
            pl.BlockSpec((CONV_W, tf), lambda i, j: (0, j)),
            pl.BlockSpec((CONV_W, tf), lambda i, j: (0, nf + j)),
            pl.BlockSpec((1, tf), lambda i, j: (0, j)),
            pl.BlockSpec((1, tf), lambda i, j: (0, nf + j)),
            pl.BlockSpec((tf, D), lambda i, j: (j, 0)),
            pl.BlockSpec((1, D), lambda i, j: (0, 0)),
        ],
        out_specs=pl.BlockSpec((tm, D), lambda i, j: (i, 0)),
        out_shape=jax.ShapeDtypeStruct((T, D), F32),
        scratch_shapes=[pltpu.VMEM((tm + halo, D), BF16), pltpu.VMEM((tm, D), F32)],
        compiler_params=_params("parallel", "arbitrary"),
    )(x, x, g, shift, scale, gate, w_up, w_up, conv_w, conv_w, conv_b, conv_b, w_down, final_g)


def _mla_q_kernel(x_ref, g_ref, sh_ref, sc_ref, wdq_ref, qg_ref, wn_ref, wr_ref, wrs_ref,
                  cos_ref, sin_ref, qn_ref, qr_ref, *, scale):
    h = _normmod(x_ref[...], g_ref[...], sh_ref[...], sc_ref[...]).astype(BF16)
    cq = _rmsnorm(_dot(h, wdq_ref[...]), qg_ref[...]).astype(BF16)
    qn_ref[...] = (_dot(cq, wn_ref[...]) * scale).astype(BF16)
    cos = jnp.concatenate([cos_ref[...]] * MLA_HEADS, axis=1)
    sin = jnp.concatenate([sin_ref[...]] * MLA_HEADS, axis=1)
    qr = _dot(cq, wr_ref[...]) * cos + _dot(cq, wrs_ref[...]) * sin
    qr_ref[...] = (qr * scale).astype(BF16)


def _mla_q(x, g, shift, scale_mod, w_dq, q_g, w_n, w_r, w_rs, cos, sin, seq, *, tm=256):
    T, D = x.shape
    R = w_dq.shape[1]
    N = w_n.shape[1]
    tm = min(tm, seq)
    per_seq = seq // tm
    batch_map = lambda i: (i // per_seq, 0, 0)
    const = lambda i: (0, 0)
    rowblk = lambda i: (i, 0)
    return pl.pallas_call(
        functools.partial(_mla_q_kernel, scale=(MLA_NOPE + MLA_ROPE) ** -0.5),
        grid=(T // tm,),
        in_specs=[
            pl.BlockSpec((tm, D), rowblk),
            pl.BlockSpec((1, D), const),
            pl.BlockSpec((None, 1, D), batch_map),
            pl.BlockSpec((None, 1, D), batch_map),
            pl.BlockSpec((D, R), const),
            pl.BlockSpec((1, R), const),
            pl.BlockSpec((R, N), const),
            pl.BlockSpec((R, N), const),
            pl.BlockSpec((R, N), const),
            pl.BlockSpec((tm, LANE), rowblk),
            pl.BlockSpec((tm, LANE), rowblk),
        ],
        out_specs=[pl.BlockSpec((tm, N), rowblk), pl.BlockSpec((tm, N), rowblk)],
        out_shape=[jax.ShapeDtypeStruct((T, N), BF16), jax.ShapeDtypeStruct((T, N), BF16)],
        compiler_params=_params("parallel"),
    )(x, g, shift, scale_mod, w_dq, q_g, w_n, w_r, w_rs, cos, sin)


def _mla_kv_kernel(x_ref, g_ref, sh_ref, sc_ref, wdkv_ref, lg_ref, wk_ref, wv_ref,
                   cos_ref, sin_ref, kn_ref, kr_ref, v_ref):
    h = _normmod(x_ref[...], g_ref[...], sh_ref[...], sc_ref[...]).astype(BF16)
    ckv = _dot(h, wdkv_ref[...])
    lat = _rmsnorm(ckv[:, :MLA_KV_RANK], lg_ref[...]).astype(BF16)
    kn_ref[...] = _dot(lat, wk_ref[...]).astype(BF16)
    v_ref[...] = _dot(lat, wv_ref[...]).astype(BF16)
    kr = (ckv[:, MLA_KV_RANK:MLA_KV_RANK + LANE] * cos_ref[...]
          + ckv[:, MLA_KV_RANK + LANE:] * sin_ref[...])
    kr_ref[...] = kr.astype(BF16)


def _mla_kv(x, g, shift, scale_mod, w_dkv, lat_g, w_k, w_v, cos, sin, seq, *, tm=256):
    T, D = x.shape
    R = w_k.shape[0]
    N = w_k.shape[1]
    tm = min(tm, seq)
    per_seq = seq // tm
    batch_map = lambda i: (i // per_seq, 0, 0)
    const = lambda i: (0, 0)
    rowblk = lambda i: (i, 0)
    return pl.pallas_call(
        _mla_kv_kernel,
        grid=(T // tm,),
        in_specs=[
            pl.BlockSpec((tm, D), rowblk),
            pl.BlockSpec((1, D), const),
            pl.BlockSpec((None, 1, D), batch_map),
            pl.BlockSpec((None, 1, D), batch_map),
            pl.BlockSpec((D, w_dkv.shape[1]), const),
            pl.BlockSpec((1, R), const),
            pl.BlockSpec((R, N), const),
            pl.BlockSpec((R, N), const),
            pl.BlockSpec((tm, LANE), rowblk),
            pl.BlockSpec((tm, LANE), rowblk),
        ],
        out_specs=[pl.BlockSpec((tm, N), rowblk), pl.BlockSpec((tm, LANE), rowblk),
                   pl.BlockSpec((tm, N), rowblk)],
        out_shape=[jax.ShapeDtypeStruct((T, N), BF16), jax.ShapeDtypeStruct((T, LANE), BF16),
                   jax.ShapeDtypeStruct((T, N), BF16)],
        compiler_params=_params("parallel"),
    )(x, g, shift, scale_mod, w_dkv, lat_g, w_k, w_v, cos, sin)


def _attn_kernel(qn_ref, qr_ref, kn_ref, kr_ref, v_ref, o_ref, *, blk):
    seq = qn_ref.shape[0]
    ones = jnp.ones((blk, LANE), BF16)
    row = lax.broadcasted_iota(jnp.int32, (blk, blk), 0)
    col = lax.broadcasted_iota(jnp.int32, (blk, blk), 1)
    for qi in range(seq // blk):
        qrows = pl.ds(qi * blk, blk)
        q = jnp.concatenate([qn_ref[qrows, :], qr_ref[qrows, :]], axis=1)
        m = jnp.full((blk, LANE), -jnp.inf, F32)
        acc = jnp.zeros((blk, 2 * LANE), F32)
        for ki in range(qi + 1):
            krows = pl.ds(ki * blk, blk)
            k = jnp.concatenate([kn_ref[krows, :], kr_ref[krows, :]], axis=1)
            s = _dot_nt(q, k)
            if ki == qi:
                s = jnp.where(col <= row, s, -jnp.inf)
            m_new = jnp.maximum(m, jnp.max(s, axis=-1, keepdims=True))
            alpha = jnp.exp(m - m_new)
            p = jnp.exp(s - jnp.concatenate([m_new] * (blk // LANE), axis=1))
            v1 = jnp.concatenate([v_ref[krows, :], ones], axis=1)
            acc = jnp.concatenate([alpha, alpha], axis=1) * acc + _dot(p.astype(BF16), v1)
            m = m_new
        o_ref[qrows, :] = (acc[:, :LANE] / acc[:, LANE:]).astype(o_ref.dtype)


def _attention(qn, qr, kn, kr, v, batch, seq, *, blk=256):
    T = qn.shape[0]
    H = MLA_HEADS
    blk = min(blk, seq)
    head = lambda b, h: (b, h)
    return pl.pallas_call(
        functools.partial(_attn_kernel, blk=blk),
        grid=(batch, H),
        in_specs=[
            pl.BlockSpec((seq, LANE), head),
            pl.BlockSpec((seq, LANE), head),
            pl.BlockSpec((seq, LANE), head),
            pl.BlockSpec((seq, LANE), lambda b, h: (b, 0)),
            pl.BlockSpec((seq, LANE), head),
        ],
        out_specs=pl.BlockSpec((seq, LANE), head),
        out_shape=jax.ShapeDtypeStruct((T, H * MLA_V), BF16),
        compiler_params=_params("parallel", "parallel"),
    )(qn, qr, kn, kr, v)


def _pad_cols(w, n):
    return jnp.pad(w, [(0, 0)] * (w.ndim - 1) + [(0, n - w.shape[-1])])


def _rotate_half_cols(w):
    half = w.shape[-1] // 2
    return jnp.concatenate([-w[..., half:], w[..., :half]], axis=-1)


def kernel(x, c, positions, mod_w, mod_b, norm_mix_g, norm_ffn_g, gla_w_in, gla_w_gate2, gla_b_gate,
           gla_head_g, gla_w_o, kv_norm_g, kv_mod_w, kv_mod_b, mla_w_dkv, mla_kv_norm_g, mla_w_ukv,
           mla_w_dq, mla_q_norm_g, mla_w_uq, mla_w_o, ffn_w_up, ffn_conv_w, ffn_conv_b, ffn_w_down,
           final_norm_g):
    B, S, D = x.shape
    T = B * S
    depth = mod_w.shape[0]
    n_gla = gla_w_in.shape[0]
    H = MLA_HEADS

    mod = _mod_linear(c, mod_w, mod_b[:, None, :]).reshape(depth, B, 6, 1, D)
    kv_mod = _mod_linear(c, kv_mod_w[None], kv_mod_b[None, None, :]).reshape(B, 2, 1, D)

    inv = ROPE_THETA ** (-jnp.arange(0, MLA_ROPE, 2, dtype=F32) / MLA_ROPE)
    ang = positions.astype(F32)[..., None] * inv
    cos = jnp.tile(jnp.cos(ang), (1, 1, LANE // (MLA_ROPE // 2))).reshape(T, LANE)
    sin = jnp.tile(jnp.sin(ang), (1, 1, LANE // (MLA_ROPE // 2))).reshape(T, LANE)

    xs = x.reshape(T, D)
    for i in range(depth):
        sh_m, sc_m, g_m, sh_f, sc_f, g_f = (mod[i, :, t] for t in range(6))
        g_mix = norm_mix_g[i][None, :]
        if i < n_gla:
            gla_cols = 2 * GLA_HEADS * GLA_DK + 2 * GLA_HEADS * GLA_DV + LANE
            w_in = _pad_cols(gla_w_in[i], gla_cols).astype(BF16)
            proj = _normmod_matmul(xs, g_mix, sh_m, sc_m, w_in, S, tm=1024, tn=896, out_dtype=BF16)
            w2 = jnp.pad(gla_w_gate2[i], ((0, LANE - GLA_GATE_RANK), (0, 0))).astype(BF16)
            y = _gla(proj, w2, gla_b_gate[i][None, :], gla_head_g[i].reshape(1, -1), B, S)
            w_o = gla_w_o[i].astype(BF16)
        else:
            j = i - n_gla
            if j == 0:
                w_dkv = mla_w_dkv
                w_kr = w_dkv[:, MLA_KV_RANK:]
                w_dkv_cat = jnp.concatenate(
                    [w_dkv[:, :MLA_KV_RANK], _pad_cols(w_kr, LANE), _pad_cols(_rotate_half_cols(w_kr), LANE)],
                    axis=1).astype(BF16)
                w_ukv = mla_w_ukv.reshape(MLA_KV_RANK, H, MLA_NOPE + MLA_V)
                w_k = w_ukv[:, :, :MLA_NOPE].reshape(MLA_KV_RANK, H * MLA_NOPE).astype(BF16)
                w_v = w_ukv[:, :, MLA_NOPE:].reshape(MLA_KV_RANK, H * MLA_V).astype(BF16)
                kn, kr, v = _mla_kv(xs, kv_norm_g[None, :], kv_mod[:, 0], kv_mod[:, 1], w_dkv_cat,
                                    mla_kv_norm_g[None, :], w_k, w_v, cos, sin, S)
            w_uq = mla_w_uq[j].reshape(-1, H, MLA_NOPE + MLA_ROPE)
            rq = w_uq.shape[0]
            w_n = w_uq[:, :, :MLA_NOPE].reshape(rq, H * LANE).astype(BF16)
            w_rope = w_uq[:, :, MLA_NOPE:]
            w_r = _pad_cols(w_rope, LANE).reshape(rq, H * LANE).astype(BF16)
            w_rs = _pad_cols(_rotate_half_cols(w_rope), LANE).reshape(rq, H * LANE).astype(BF16)
            qn, qr = _mla_q(xs, g_mix, sh_m, sc_m, mla_w_dq[j].astype(BF16), mla_q_norm_g[j][None, :],
                            w_n, w_r, w_rs, cos, sin, S)
            y = _attention(qn, qr, kn, kr, v, B, S)
            w_o = mla_w_o[j].astype(BF16)
        xs = _matmul_residual(y, w_o, xs, g_m, S)
        xs = _conv_ffn(xs, norm_ffn_g[i][None, :], sh_f, sc_f, g_f, ffn_w_up[i].astype(BF16),
                       ffn_conv_w[i], ffn_conv_b[i][None, :], ffn_w_down[i].astype(BF16),
                       final_norm_g[None, :], S, final_norm=(i == depth - 1))
    return xs.reshape(B, S, D)
```

```python
import functools

import jax
import jax.numpy as jnp
import numpy as np
from jax import lax
from jax.experimental import pallas as pl
from jax.experimental.pallas import tpu as pltpu

F32 = jnp.float32
BF16 = jnp.bfloat16

EPS = 1e-6
LANE = 128
SUBLANES = 8
BF16_ROWS = 16
VMEM_LIMIT_BYTES = 56 << 20
FFN_VMEM_LIMIT_BYTES = 60 << 20

GLA_HEADS = 4
GLA_DK = 256
GLA_DV = 512
GLA_GATE_RANK = 16
GLA_GATE_NORM = 16.0
GLA_CHUNK = 64
GLA_LEVELS = 6
GLA_MIN_BCAST = 4

MLA_HEADS = 16
MLA_NOPE = 128
MLA_ROPE = 64
MLA_V = 128
MLA_KV_RANK = 512
ROPE_THETA = 10000.0

CONV_W = 3


def _params(*sem, vmem_limit_bytes=VMEM_LIMIT_BYTES):
    return pltpu.CompilerParams(dimension_semantics=sem, vmem_limit_bytes=vmem_limit_bytes)


def _dot(a, b):
    return jnp.dot(a, b, preferred_element_type=F32)


def _dot_nt(a, b):
    return lax.dot_general(a, b, (((1,), (1,)), ((), ())), preferred_element_type=F32)


def _dot_tn(a, b):
    return lax.dot_general(a, b, (((0,), (0,)), ((), ())), preferred_element_type=F32)


def _sigmoid(x):
    return 1.0 / (1.0 + jnp.exp(-x))


def _rmsnorm(x, g):
    return x * lax.rsqrt(jnp.mean(x * x, axis=-1, keepdims=True) + EPS) * g


def _normmod(x, g, shift, scale):
    return _rmsnorm(x, g) * (1.0 + scale) + shift


def _mod_kernel(c_ref, w_ref, b_ref, o_ref):
    c = c_ref[...]
    c_act = (c * _sigmoid(c)).astype(BF16)
    o_ref[...] = _dot(c_act, w_ref[...].astype(BF16)) + b_ref[...]


def _mod_linear(c, w, b, *, tn=1024):
    L, D, N = w.shape
    B = c.shape[0]
    return pl.pallas_call(
        _mod_kernel,
        grid=(L, N // tn),
        in_specs=[
            pl.BlockSpec((B, D), lambda l, j: (0, 0)),
            pl.BlockSpec((None, D, tn), lambda l, j: (l, 0, j)),
            pl.BlockSpec((None, 1, tn), lambda l, j: (l, 0, j)),
        ],
        out_specs=pl.BlockSpec((None, B, tn), lambda l, j: (l, 0, j)),
        out_shape=jax.ShapeDtypeStruct((L, B, N), F32),
        compiler_params=_params("parallel", "parallel"),
    )(c, w, b)


def _normmod_matmul_kernel(x_ref, g_ref, sh_ref, sc_ref, w_ref, o_ref, hn_ref):
    @pl.when(pl.program_id(1) == 0)
    def _():
        hn_ref[...] = _normmod(x_ref[...], g_ref[...], sh_ref[...], sc_ref[...]).astype(BF16)

    o_ref[...] = _dot(hn_ref[...], w_ref[...]).astype(o_ref.dtype)


def _normmod_matmul(x, g, shift, scale, w, seq, *, tm, tn, out_dtype):
    T, D = x.shape
    N = w.shape[1]
    tm = min(tm, seq)
    per_seq = seq // tm
    return pl.pallas_call(
        _normmod_matmul_kernel,
        grid=(T // tm, N // tn),
        in_specs=[
            pl.BlockSpec((tm, D), lambda i, j: (i, 0)),
            pl.BlockSpec((1, D), lambda i, j: (0, 0)),
            pl.BlockSpec((None, 1, D), lambda i, j: (i // per_seq, 0, 0)),
            pl.BlockSpec((None, 1, D), lambda i, j: (i // per_seq, 0, 0)),
            pl.BlockSpec((D, tn), lambda i, j: (0, j)),
        ],
        out_specs=pl.BlockSpec((tm, tn), lambda i, j: (i, j)),
        out_shape=jax.ShapeDtypeStruct((T, N), out_dtype),
        scratch_shapes=[pltpu.VMEM((tm, D), BF16)],
        compiler_params=_params("parallel", "arbitrary"),
    )(x, g, shift, scale, w)


def _matmul_residual_kernel(a_ref, w_ref, res_ref, gate_ref, o_ref):
    o_ref[...] = res_ref[...] + (1.0 + gate_ref[...]) * _dot(a_ref[...], w_ref[...])


def _matmul_residual(a, w, res, gate, seq, *, tm=512):
    T, K = a.shape
    N = w.shape[1]
    tn = N
    tm = min(tm, seq)
    per_seq = seq // tm
    return pl.pallas_call(
        _matmul_residual_kernel,
        grid=(T // tm, N // tn),
        in_specs=[
            pl.BlockSpec((tm, K), lambda i, j: (i, 0)),
            pl.BlockSpec((K, tn), lambda i, j: (0, j)),
            pl.BlockSpec((tm, tn), lambda i, j: (i, j)),
            pl.BlockSpec((None, 1, tn), lambda i, j: (i // per_seq, 0, j)),
        ],
        out_specs=pl.BlockSpec((tm, tn), lambda i, j: (i, j)),
        out_shape=jax.ShapeDtypeStruct((T, N), F32),
        compiler_params=_params("parallel", "parallel"),
    )(a, w, res, gate)


def _gla_tables():
    C = GLA_CHUNK
    r = np.arange(C)
    tri = (r[None, :] <= r[:, None]).astype(np.float32)
    mats, sgn = [tri], []
    level = np.full((C, C), -1, np.int32)
    level[r, r] = GLA_LEVELS
    for l in range(GLA_LEVELS):
        h = C >> (l + 1)
        if h < GLA_MIN_BCAST:
            mats.append(tri[(r // (2 * h)) * (2 * h) + h])
        is_q = (r // h) % 2 == 1
        sgn.append(np.where(is_q, 1.0, -1.0))
        level[is_q[:, None] & ((r[None, :] // h) == (r[:, None] // h) - 1)] = l
    mstack = jnp.asarray(np.concatenate(mats, axis=0), BF16)
    sgn = jnp.asarray(np.repeat(np.concatenate(sgn)[:, None], GLA_DK, axis=1), F32)
    return mstack, sgn, jnp.asarray(level)


def _gla_decays(g1, w2, b_gate, mstack):
    z = _dot(g1, w2) + b_gate
    log_a = (jnp.minimum(z, 0.0) - jnp.log(1.0 + jnp.exp(-jnp.abs(z)))) * (1.0 / GLA_GATE_NORM)
    la_hi = log_a.astype(BF16)
    la_lo = (log_a - la_hi.astype(F32)).astype(BF16)
    return _dot(mstack, la_hi) + _dot(mstack, la_lo)


def _gla_local(q, k, v, bs, sgn_ref, level):
    C, DK = q.shape
    b = bs[:C]
    b_last = b[C - 1:C, :]
    n_fine = bs.shape[0] // C - 1
    scores = jnp.where(level == GLA_LEVELS, jnp.sum(q * k, axis=-1, keepdims=True), 0.0)
    for l in range(GLA_LEVELS):
        sg = sgn_ref[l * C:(l + 1) * C, :]
        h = C >> (l + 1)
        if h >= GLA_MIN_BCAST:
            b_ref = jnp.concatenate(
                [jnp.broadcast_to(b[r0 + h:r0 + h + 1, :], (2 * h, DK)) for r0 in range(0, C, 2 * h)], axis=0)
        else:
            fine = l - (GLA_LEVELS - n_fine)
            b_ref = bs[(fine + 1) * C:(fine + 2) * C]
        t = (jnp.where(sg > 0.0, q, k) * jnp.exp((b - b_ref) * sg)).astype(BF16)
        scores = jnp.where(level == l, _dot_nt(t, t), scores)
    intra = _dot(scores.astype(BF16), v)
    q_dec = (q * jnp.exp(b)).astype(BF16)
    k_dec = (k * jnp.exp(b_last - b)).astype(BF16)
    update = _dot_tn(k_dec, v)
    decay_col = jnp.exp(jnp.transpose(jnp.broadcast_to(b_last, (LANE, DK))))
    return intra, q_dec, update, decay_col


def _gla_kernel(q_ref, k_ref, v_ref, r_ref, g1_ref, w2_ref, bg_ref, hg_ref, ms_ref, sgn_ref, lv_ref,
                o_ref, state_ref, *, rows):
    @pl.when(pl.program_id(2) == 0)
    def _():
        state_ref[...] = jnp.zeros_like(state_ref)

    w2 = w2_ref[...]
    b_gate = bg_ref[...]
    head_g = hg_ref[...]
    mstack = ms_ref[...]
    level = lv_ref[...]
    chunks = [pl.ds(c * GLA_CHUNK, GLA_CHUNK) for c in range(rows // GLA_CHUNK)]
    decays = [_gla_decays(g1_ref[sl, :], w2, b_gate, mstack) for sl in chunks]
    local = []
    for sl, bs in zip(chunks, decays):
        q = q_ref[sl, :].astype(F32) * (GLA_DK ** -0.5)
        local.append(_gla_local(q, k_ref[sl, :].astype(F32), v_ref[sl, :], bs, sgn_ref, level))
    state = state_ref[...]
    for sl, (intra, q_dec, update, decay_col) in zip(chunks, local):
        o = _dot(q_dec, state.astype(BF16)) + intra
        state = jnp.concatenate([decay_col] * (state.shape[1] // LANE), axis=1) * state + update
        r = r_ref[sl, :].astype(F32)
        o_ref[sl, :] = (_rmsnorm(o, head_g) * (r * _sigmoid(r))).astype(o_ref.dtype)
    state_ref[...] = state


def _gla(proj, w2, b_gate, head_g, batch, seq, *, rows=512):
    T = proj.shape[0]
    H, DK, DV = GLA_HEADS, GLA_DK, GLA_DV
    rows = min(rows, seq)
    nc = seq // rows
    rowmap = lambda b, h, c: b * nc + c
    const = lambda b, h, c: (0, 0)
    v_off = 2 * H * DK // DV
    r_off = v_off + H
    g_off = (2 * H * DK + 2 * H * DV) // LANE
    mstack, sgn, level = _gla_tables()
    return pl.pallas_call(
        functools.partial(_gla_kernel, rows=rows),
        grid=(batch, H, nc),
        in_specs=[
            pl.BlockSpec((rows, DK), lambda b, h, c: (rowmap(b, h, c), h)),
            pl.BlockSpec((rows, DK), lambda b, h, c: (rowmap(b, h, c), H + h)),
            pl.BlockSpec((rows, DV), lambda b, h, c: (rowmap(b, h, c), v_off + h)),
            pl.BlockSpec((rows, DV), lambda b, h, c: (rowmap(b, h, c), r_off + h)),
            pl.BlockSpec((rows, LANE), lambda b, h, c: (rowmap(b, h, c), g_off)),
            pl.BlockSpec((LANE, DK), lambda b, h, c: (0, h)),
            pl.BlockSpec((1, DK), lambda b, h, c: (0, h)),
            pl.BlockSpec((1, DV), lambda b, h, c: (0, h)),
            pl.BlockSpec(mstack.shape, const),
            pl.BlockSpec(sgn.shape, const),
            pl.BlockSpec(level.shape, const),
        ],
        out_specs=pl.BlockSpec((rows, DV), lambda b, h, c: (rowmap(b, h, c), h)),
        out_shape=jax.ShapeDtypeStruct((T, H * DV), BF16),
        scratch_shapes=[pltpu.VMEM((DK, DV), F32)],
        compiler_params=_params("parallel", "parallel", "arbitrary"),
    )(proj, proj, proj, proj, proj, w2, b_gate, head_g, mstack, sgn, level)


def _ffn_kernel(x_ref, xh_ref, g_ref, sh_ref, sc_ref, gate_ref, wv_ref, wg_ref, cwv_ref, cwg_ref,
                cbv_ref, cbg_ref, wd_ref, fg_ref, o_ref, hn_ref, acc_ref, *, tm, per_seq, final_norm):
    i = pl.program_id(0)
    j = pl.program_id(1)
    halo = BF16_ROWS
    sub = SUBLANES
    span = tm // sub
    d_model = x_ref.shape[1]

    @pl.when(j == 0)
    def _():
        g, sh, sc = g_ref[...], sh_ref[...], sc_ref[...]
        h = _normmod(x_ref[...], g, sh, sc)
        h = pltpu.einshape("svd->vsd", h.reshape(sub, span, d_model)).reshape(tm, d_model)
        hn_ref[halo:, :] = h.astype(BF16)
        prev = _normmod(xh_ref[...], g, sh, sc)
        hn_ref[:halo, :] = jnp.where(i % per_seq == 0, 0.0, prev).astype(BF16)
        acc_ref[...] = jnp.zeros_like(acc_ref)

    hn = hn_ref[...]
    sub_id = lax.broadcasted_iota(jnp.int32, (sub, wv_ref.shape[1]), 0)

    def conv(u, cw_ref, cb_ref):
        cw = cw_ref[...]
        um = u[halo:]
        tail = u[halo - sub:halo]
        first1 = pltpu.roll(jnp.where(sub_id == sub - 1, tail, um[tm - sub:]), 1, 0)
        first2 = pltpu.roll(
            jnp.where(sub_id == sub - 1, pltpu.roll(tail, 1, 0), um[tm - 2 * sub:tm - sub]), 1, 0)
        back1 = jnp.concatenate([first1, um[:tm - sub]], axis=0)
        back2 = jnp.concatenate([first2, first1, um[:tm - 2 * sub]], axis=0)
        return cb_ref[...] + cw[2:3, :] * um + cw[1:2, :] * back1 + cw[0:1, :] * back2

    val = conv(_dot(hn, wv_ref[...]), cwv_ref, cbv_ref)
    gt = conv(_dot(hn, wg_ref[...]), cwg_ref, cbg_ref)
    act = (gt * _sigmoid(gt) * val).astype(BF16)
    acc_ref[...] += _dot(act, wd_ref[...])

    @pl.when(j == pl.num_programs(1) - 1)
    def _():
        acc = pltpu.einshape("vsd->svd", acc_ref[...].reshape(span, sub, d_model)).reshape(tm, d_model)
        y = x_ref[...] + (1.0 + gate_ref[...]) * acc
        if final_norm:
            y = _rmsnorm(y, fg_ref[...])
        o_ref[...] = y


def _conv_ffn(x, g, shift, scale, gate, w_up, conv_w, conv_b, w_down, final_g, seq, layer, *,
              final_norm, tm=1024, tf=512):
    assert conv_w.shape[1] == CONV_W == 3
    T, D = x.shape
    F = w_down.shape[1]
    tm = min(tm, seq)
    per_seq = seq // tm
    nf = F // tf
    halo = BF16_ROWS
    hb = tm // halo
    batch_map = lambda i, j: (i // per_seq, 0, 0)
    return pl.pallas_call(
        functools.partial(_ffn_kernel, tm=tm, per_seq=per_seq, final_norm=final_norm),
        grid=(T // tm, nf),
        in_specs=[
            pl.BlockSpec((tm, D), lambda i, j: (i, 0), pipeline_mode=pl.Buffered(1)),
            pl.BlockSpec((halo, D), lambda i, j: (jnp.maximum(i * hb - 1, 0), 0)),
            pl.BlockSpec((1, D), lambda i, j: (0, 0)),
            pl.BlockSpec((None, 1, D), batch_map),
            pl.BlockSpec((None, 1, D), batch_map),
            pl.BlockSpec((None, 1, D), batch_map),
            pl.BlockSpec((None, D, tf), lambda i, j: (layer, 0, j)),
            pl.BlockSpec((None, D, tf), lambda i, j: (layer, 0, nf + j)),
            pl.BlockSpec((None, CONV_W, tf), lambda i, j: (layer, 0, j)),
            pl.BlockSpec((None, CONV_W, tf), lambda i, j: (layer, 0, nf + j)),
            pl.BlockSpec((None, 1, tf), lambda i, j: (layer, 0, j)),
            pl.BlockSpec((None, 1, tf), lambda i, j: (layer, 0, nf + j)),
            pl.BlockSpec((None, tf, D), lambda i, j: (layer, j, 0)),
            pl.BlockSpec((1, D), lambda i, j: (0, 0)),
        ],
        out_specs=pl.BlockSpec((tm, D), lambda i, j: (i, 0), pipeline_mode=pl.Buffered(1)),
        out_shape=jax.ShapeDtypeStruct((T, D), F32),
        scratch_shapes=[pltpu.VMEM((tm + halo, D), BF16), pltpu.VMEM((tm, D), F32)],
        compiler_params=_params("parallel", "arbitrary", vmem_limit_bytes=FFN_VMEM_LIMIT_BYTES),
    )(x, x, g, shift, scale, gate, w_up, w_up, conv_w, conv_w, conv_b, conv_b, w_down, final_g)


def _mla_q_kernel(x_ref, g_ref, sh_ref, sc_ref, wdq_ref, qg_ref, wn_ref, wr_ref, wrs_ref,
                  cos_ref, sin_ref, qn_ref, qr_ref, *, scale):
    h = _normmod(x_ref[...], g_ref[...], sh_ref[...], sc_ref[...]).astype(BF16)
    cq = _rmsnorm(_dot(h, wdq_ref[...]), qg_ref[...]).astype(BF16)
    qn_ref[...] = (_dot(cq, wn_ref[...]) * scale).astype(BF16)
    cos = jnp.concatenate([cos_ref[...]] * MLA_HEADS, axis=1)
    sin = jnp.concatenate([sin_ref[...]] * MLA_HEADS, axis=1)
    qr = _dot(cq, wr_ref[...]) * cos + _dot(cq, wrs_ref[...]) * sin
    qr_ref[...] = (qr * scale).astype(BF16)


def _mla_q(x, g, shift, scale_mod, w_dq, q_g, w_n, w_r, w_rs, cos, sin, seq, *, tm=256):
    T, D = x.shape
    R = w_dq.shape[1]
    N = w_n.shape[1]
    tm = min(tm, seq)
    per_seq = seq // tm
    batch_map = lambda i: (i // per_seq, 0, 0)
    const = lambda i: (0, 0)
    rowblk = lambda i: (i, 0)
    return pl.pallas_call(
        functools.partial(_mla_q_kernel, scale=(MLA_NOPE + MLA_ROPE) ** -0.5),
        grid=(T // tm,),
        in_specs=[
            pl.BlockSpec((tm, D), rowblk),
            pl.BlockSpec((1, D), const),
            pl.BlockSpec((None, 1, D), batch_map),
            pl.BlockSpec((None, 1, D), batch_map),
            pl.BlockSpec((D, R), const),
            pl.BlockSpec((1, R), const),
            pl.BlockSpec((R, N), const),
            pl.BlockSpec((R, N), const),
            pl.BlockSpec((R, N), const),
            pl.BlockSpec((tm, LANE), rowblk),
            pl.BlockSpec((tm, LANE), rowblk),
        ],
        out_specs=[pl.BlockSpec((tm, N), rowblk), pl.BlockSpec((tm, N), rowblk)],
        out_shape=[jax.ShapeDtypeStruct((T, N), BF16), jax.ShapeDtypeStruct((T, N), BF16)],
        compiler_params=_params("parallel"),
    )(x, g, shift, scale_mod, w_dq, q_g, w_n, w_r, w_rs, cos, sin)


def _mla_kv_kernel(x_ref, g_ref, sh_ref, sc_ref, wdkv_ref, lg_ref, wk_ref, wv_ref,
                   cos_ref, sin_ref, kn_ref, kr_ref, v_ref):
    h = _normmod(x_ref[...], g_ref[...], sh_ref[...], sc_ref[...]).astype(BF16)
    ckv = _dot(h, wdkv_ref[...])
    lat = _rmsnorm(ckv[:, :MLA_KV_RANK], lg_ref[...]).astype(BF16)
    kn_ref[...] = _dot(lat, wk_ref[...]).astype(BF16)
    v_ref[...] = _dot(lat, wv_ref[...]).astype(BF16)
    kr = (ckv[:, MLA_KV_RANK:MLA_KV_RANK + LANE] * cos_ref[...]
          + ckv[:, MLA_KV_RANK + LANE:] * sin_ref[...])
    kr_ref[...] = kr.astype(BF16)


def _mla_kv(x, g, shift, scale_mod, w_dkv, lat_g, w_k, w_v, cos, sin, seq, *, tm=256):
    T, D = x.shape
    R = w_k.shape[0]
    N = w_k.shape[1]
    tm = min(tm, seq)
    per_seq = seq // tm
    batch_map = lambda i: (i // per_seq, 0, 0)
    const = lambda i: (0, 0)
    rowblk = lambda i: (i, 0)
    return pl.pallas_call(
        _mla_kv_kernel,
        grid=(T // tm,),
        in_specs=[
            pl.BlockSpec((tm, D), rowblk),
            pl.BlockSpec((1, D), const),
            pl.BlockSpec((None, 1, D), batch_map),
            pl.BlockSpec((None, 1, D), batch_map),
            pl.BlockSpec((D, w_dkv.shape[1]), const),
            pl.BlockSpec((1, R), const),
            pl.BlockSpec((R, N), const),
            pl.BlockSpec((R, N), const),
            pl.BlockSpec((tm, LANE), rowblk),
            pl.BlockSpec((tm, LANE), rowblk),
        ],
        out_specs=[pl.BlockSpec((tm, N), rowblk), pl.BlockSpec((tm, LANE), rowblk),
                   pl.BlockSpec((tm, N), rowblk)],
        out_shape=[jax.ShapeDtypeStruct((T, N), BF16), jax.ShapeDtypeStruct((T, LANE), BF16),
                   jax.ShapeDtypeStruct((T, N), BF16)],
        compiler_params=_params("parallel"),
    )(x, g, shift, scale_mod, w_dkv, lat_g, w_k, w_v, cos, sin)


def _attn_kernel(qn_ref, qr_ref, kn_ref, kr_ref, v_ref, o_ref, *, blk):
    seq = qn_ref.shape[0]
    ones = jnp.ones((blk, LANE), BF16)
    row = lax.broadcasted_iota(jnp.int32, (blk, blk), 0)
    col = lax.broadcasted_iota(jnp.int32, (blk, blk), 1)
    for qi in range(seq // blk):
        qrows = pl.ds(qi * blk, blk)
        q = jnp.concatenate([qn_ref[qrows, :], qr_ref[qrows, :]], axis=1)
        m = jnp.full((blk, LANE), -jnp.inf, F32)
        acc = jnp.zeros((blk, 2 * LANE), F32)
        for ki in range(qi + 1):
            krows = pl.ds(ki * blk, blk)
            k = jnp.concatenate([kn_ref[krows, :], kr_ref[krows, :]], axis=1)
            s = _dot_nt(q, k)
            if ki == qi:
                s = jnp.where(col <= row, s, -jnp.inf)
            m_new = jnp.maximum(m, jnp.max(s, axis=-1, keepdims=True))
            alpha = jnp.exp(m - m_new)
            p = jnp.exp(s - jnp.concatenate([m_new] * (blk // LANE), axis=1))
            v1 = jnp.concatenate([v_ref[krows, :], ones], axis=1)
            acc = jnp.concatenate([alpha, alpha], axis=1) * acc + _dot(p.astype(BF16), v1)
            m = m_new
        o_ref[qrows, :] = (acc[:, :LANE] / acc[:, LANE:]).astype(o_ref.dtype)


def _attention(qn, qr, kn, kr, v, batch, seq, *, blk=256):
    T = qn.shape[0]
    H = MLA_HEADS
    blk = min(blk, seq)
    head = lambda b, h: (b, h)
    return pl.pallas_call(
        functools.partial(_attn_kernel, blk=blk),
        grid=(batch, H),
        in_specs=[
            pl.BlockSpec((seq, LANE), head),
            pl.BlockSpec((seq, LANE), head),
            pl.BlockSpec((seq, LANE), head),
            pl.BlockSpec((seq, LANE), lambda b, h: (b, 0)),
            pl.BlockSpec((seq, LANE), head),
        ],
        out_specs=pl.BlockSpec((seq, LANE), head),
        out_shape=jax.ShapeDtypeStruct((T, H * MLA_V), BF16),
        compiler_params=_params("parallel", "parallel"),
    )(qn, qr, kn, kr, v)


def _pad_cols(w, n):
    return jnp.pad(w, [(0, 0)] * (w.ndim - 1) + [(0, n - w.shape[-1])])


def _rotate_half_cols(w):
    half = w.shape[-1] // 2
    return jnp.concatenate([-w[..., half:], w[..., :half]], axis=-1)


def kernel(x, c, positions, mod_w, mod_b, norm_mix_g, norm_ffn_g, gla_w_in, gla_w_gate2, gla_b_gate,
           gla_head_g, gla_w_o, kv_norm_g, kv_mod_w, kv_mod_b, mla_w_dkv, mla_kv_norm_g, mla_w_ukv,
           mla_w_dq, mla_q_norm_g, mla_w_uq, mla_w_o, ffn_w_up, ffn_conv_w, ffn_conv_b, ffn_w_down,
           final_norm_g):
    B, S, D = x.shape
    T = B * S
    depth = mod_w.shape[0]
    n_gla = gla_w_in.shape[0]
    H = MLA_HEADS

    mod = _mod_linear(c, mod_w, mod_b[:, None, :]).reshape(depth, B, 6, 1, D)
    kv_mod = _mod_linear(c, kv_mod_w[None], kv_mod_b[None, None, :]).reshape(B, 2, 1, D)

    inv = ROPE_THETA ** (-jnp.arange(0, MLA_ROPE, 2, dtype=F32) / MLA_ROPE)
    ang = positions.astype(F32)[..., None] * inv
    cos = jnp.tile(jnp.cos(ang), (1, 1, LANE // (MLA_ROPE // 2))).reshape(T, LANE)
    sin = jnp.tile(jnp.sin(ang), (1, 1, LANE // (MLA_ROPE // 2))).reshape(T, LANE)

    ffn_w_up_bf = ffn_w_up.astype(BF16)
    ffn_w_down_bf = ffn_w_down.astype(BF16)

    xs = x.reshape(T, D)
    for i in range(depth):
        sh_m, sc_m, g_m, sh_f, sc_f, g_f = (mod[i, :, t] for t in range(6))
        g_mix = norm_mix_g[i][None, :]
        if i < n_gla:
            gla_cols = 2 * GLA_HEADS * GLA_DK + 2 * GLA_HEADS * GLA_DV + LANE
            w_in = _pad_cols(gla_w_in[i], gla_cols).astype(BF16)
            proj = _normmod_matmul(xs, g_mix, sh_m, sc_m, w_in, S, tm=1024, tn=896, out_dtype=BF16)
            w2 = jnp.pad(gla_w_gate2[i], ((0, LANE - GLA_GATE_RANK), (0, 0))).astype(BF16)
            y = _gla(proj, w2, gla_b_gate[i][None, :], gla_head_g[i].reshape(1, -1), B, S)
            w_o = gla_w_o[i].astype(BF16)
        else:
            j = i - n_gla
            if j == 0:
                w_dkv = mla_w_dkv
                w_kr = w_dkv[:, MLA_KV_RANK:]
                w_dkv_cat = jnp.concatenate(
                    [w_dkv[:, :MLA_KV_RANK], _pad_cols(w_kr, LANE), _pad_cols(_rotate_half_cols(w_kr), LANE)],
                    axis=1).astype(BF16)
                w_ukv = mla_w_ukv.reshape(MLA_KV_RANK, H, MLA_NOPE + MLA_V)
                w_k = w_ukv[:, :, :MLA_NOPE].reshape(MLA_KV_RANK, H * MLA_NOPE).astype(BF16)
                w_v = w_ukv[:, :, MLA_NOPE:].reshape(MLA_KV_RANK, H * MLA_V).astype(BF16)
                kn, kr, v = _mla_kv(xs, kv_norm_g[None, :], kv_mod[:, 0], kv_mod[:, 1], w_dkv_cat,
                                    mla_kv_norm_g[None, :], w_k, w_v, cos, sin, S)
            w_uq = mla_w_uq[j].reshape(-1, H, MLA_NOPE + MLA_ROPE)
            rq = w_uq.shape[0]
            w_n = w_uq[:, :, :MLA_NOPE].reshape(rq, H * LANE).astype(BF16)
            w_rope = w_uq[:, :, MLA_NOPE:]
            w_r = _pad_cols(w_rope, LANE).reshape(rq, H * LANE).astype(BF16)
            w_rs = _pad_cols(_rotate_half_cols(w_rope), LANE).reshape(rq, H * LANE).astype(BF16)
            qn, qr = _mla_q(xs, g_mix, sh_m, sc_m, mla_w_dq[j].astype(BF16), mla_q_norm_g[j][None, :],
                            w_n, w_r, w_rs, cos, sin, S)
            y = _attention(qn, qr, kn, kr, v, B, S)
            w_o = mla_w_o[j].astype(BF16)
        xs = _matmul_residual(y, w_o, xs, g_m, S)
        xs = _conv_ffn(xs, norm_ffn_g[i][None, :], sh_f, sc_f, g_f, ffn_w_up_bf, ffn_conv_w,
                       ffn_conv_b[:, None, :], ffn_w_down_bf, final_norm_g[None, :], S, i,
                       final_norm=(i == depth - 1))
    return xs.reshape(B, S, D)
```

```python
import functools

import jax
import jax.numpy as jnp
import numpy as np
from jax import lax
from jax.experimental import pallas as pl
from jax.experimental.pallas import tpu as pltpu

F32 = jnp.float32
BF16 = jnp.bfloat16

EPS = 1e-6
LANE = 128
SUBLANES = 8
BF16_ROWS = 16
VMEM_LIMIT_BYTES = 56 << 20

GLA_HEADS = 4
GLA_DK = 256
GLA_DV = 512
GLA_GATE_RANK = 16
GLA_GATE_NORM = 16.0
GLA_CHUNK = 64
GLA_LEVELS = 6
GLA_MIN_BCAST = 4

MLA_HEADS = 16
MLA_NOPE = 128
MLA_ROPE = 64
MLA_V = 128
MLA_KV_RANK = 512
ROPE_THETA = 10000.0

CONV_W = 3


def _params(*sem):
    return pltpu.CompilerParams(dimension_semantics=sem, vmem_limit_bytes=VMEM_LIMIT_BYTES)


def _dot(a, b):
    return jnp.dot(a, b, preferred_element_type=F32)


def _dot_nt(a, b):
    return lax.dot_general(a, b, (((1,), (1,)), ((), ())), preferred_element_type=F32)


def _dot_tn(a, b):
    return lax.dot_general(a, b, (((0,), (0,)), ((), ())), preferred_element_type=F32)


def _sigmoid(x):
    return 1.0 / (1.0 + jnp.exp(-x))


def _rmsnorm(x, g):
    return x * lax.rsqrt(jnp.mean(x * x, axis=-1, keepdims=True) + EPS) * g


def _normmod(x, g, shift, scale):
    return _rmsnorm(x, g) * (1.0 + scale) + shift


def _mod_kernel(c_ref, w_ref, b_ref, o_ref):
    c = c_ref[...]
    c_act = (c * _sigmoid(c)).astype(BF16)
    o_ref[...] = _dot(c_act, w_ref[...].astype(BF16)) + b_ref[...]


def _mod_linear(c, w, b, *, tn=1024):
    L, D, N = w.shape
    B = c.shape[0]
    return pl.pallas_call(
        _mod_kernel,
        grid=(L, N // tn),
        in_specs=[
            pl.BlockSpec((B, D), lambda l, j: (0, 0)),
            pl.BlockSpec((None, D, tn), lambda l, j: (l, 0, j)),
            pl.BlockSpec((None, 1, tn), lambda l, j: (l, 0, j)),
        ],
        out_specs=pl.BlockSpec((None, B, tn), lambda l, j: (l, 0, j)),
        out_shape=jax.ShapeDtypeStruct((L, B, N), F32),
        compiler_params=_params("parallel", "parallel"),
    )(c, w, b)


def _normmod_matmul_kernel(x_ref, g_ref, sh_ref, sc_ref, w_ref, ws_ref, o_ref, os_ref, hn_ref):
    @pl.when(pl.program_id(1) == 0)
    def _():
        hn_ref[...] = _normmod(x_ref[...], g_ref[...], sh_ref[...], sc_ref[...]).astype(BF16)
        os_ref[...] = _dot(hn_ref[...], ws_ref[...]).astype(os_ref.dtype)

    o_ref[...] = _dot(hn_ref[...], w_ref[...]).astype(o_ref.dtype)


def _normmod_matmul(x, g, shift, scale, w, w_side, seq, *, tm, tn, out_dtype):
    T, D = x.shape
    N = w.shape[1]
    tm = min(tm, seq)
    per_seq = seq // tm
    return pl.pallas_call(
        _normmod_matmul_kernel,
        grid=(T // tm, N // tn),
        in_specs=[
            pl.BlockSpec((tm, D), lambda i, j: (i, 0)),
            pl.BlockSpec((1, D), lambda i, j: (0, 0)),
            pl.BlockSpec((None, 1, D), lambda i, j: (i // per_seq, 0, 0)),
            pl.BlockSpec((None, 1, D), lambda i, j: (i // per_seq, 0, 0)),
            pl.BlockSpec((D, tn), lambda i, j: (0, j)),
            pl.BlockSpec((D, LANE), lambda i, j: (0, 0)),
        ],
        out_specs=[pl.BlockSpec((tm, tn), lambda i, j: (i, j)),
                   pl.BlockSpec((tm, LANE), lambda i, j: (i, 0))],
        out_shape=[jax.ShapeDtypeStruct((T, N), out_dtype), jax.ShapeDtypeStruct((T, LANE), out_dtype)],
        scratch_shapes=[pltpu.VMEM((tm, D), BF16)],
        compiler_params=_params("parallel", "arbitrary"),
    )(x, g, shift, scale, w, w_side)


def _matmul_residual_kernel(a_ref, w_ref, res_ref, gate_ref, o_ref):
    o_ref[...] = res_ref[...] + (1.0 + gate_ref[...]) * _dot(a_ref[...], w_ref[...])


def _matmul_residual(a, w, res, gate, seq, *, tm=512):
    T, K = a.shape
    N = w.shape[1]
    tn = N
    tm = min(tm, seq)
    per_seq = seq // tm
    return pl.pallas_call(
        _matmul_residual_kernel,
        grid=(T // tm, N // tn),
        in_specs=[
            pl.BlockSpec((tm, K), lambda i, j: (i, 0)),
            pl.BlockSpec((K, tn), lambda i, j: (0, j)),
            pl.BlockSpec((tm, tn), lambda i, j: (i, j)),
            pl.BlockSpec((None, 1, tn), lambda i, j: (i // per_seq, 0, j)),
        ],
        out_specs=pl.BlockSpec((tm, tn), lambda i, j: (i, j)),
        out_shape=jax.ShapeDtypeStruct((T, N), F32),
        compiler_params=_params("parallel", "parallel"),
    )(a, w, res, gate)


def _gla_tables():
    C = GLA_CHUNK
    r = np.arange(C)
    tri = (r[None, :] <= r[:, None]).astype(np.float32)
    mats, sgn = [tri], []
    level = np.full((C, C), -1, np.int32)
    level[r, r] = GLA_LEVELS
    for l in range(GLA_LEVELS):
        h = C >> (l + 1)
        if h < GLA_MIN_BCAST:
            mats.append(tri[(r // (2 * h)) * (2 * h) + h])
        is_q = (r // h) % 2 == 1
        sgn.append(np.where(is_q, 1.0, -1.0))
        level[is_q[:, None] & ((r[None, :] // h) == (r[:, None] // h) - 1)] = l
    mstack = jnp.asarray(np.concatenate(mats, axis=0), BF16)
    sgn = jnp.asarray(np.repeat(np.concatenate(sgn)[:, None], GLA_DK, axis=1), F32)
    return mstack, sgn, jnp.asarray(level)


def _gla_decays(g1, w2, b_gate, mstack):
    z = _dot(g1, w2) + b_gate
    log_a = (jnp.minimum(z, 0.0) - jnp.log(1.0 + jnp.exp(-jnp.abs(z)))) * (1.0 / GLA_GATE_NORM)
    la_hi = log_a.astype(BF16)
    la_lo = (log_a - la_hi.astype(F32)).astype(BF16)
    return _dot(mstack, la_hi) + _dot(mstack, la_lo)


def _gla_local(q, k, v, bs, sgn_ref, level):
    C, DK = q.shape
    b = bs[:C]
    b_last = b[C - 1:C, :]
    n_fine = bs.shape[0] // C - 1
    scores = jnp.where(level == GLA_LEVELS, jnp.sum(q * k, axis=-1, keepdims=True), 0.0)
    for l in range(GLA_LEVELS):
        sg = sgn_ref[l * C:(l + 1) * C, :]
        h = C >> (l + 1)
        if h >= GLA_MIN_BCAST:
            b_ref = jnp.concatenate(
                [jnp.broadcast_to(b[r0 + h:r0 + h + 1, :], (2 * h, DK)) for r0 in range(0, C, 2 * h)], axis=0)
        else:
            fine = l - (GLA_LEVELS - n_fine)
            b_ref = bs[(fine + 1) * C:(fine + 2) * C]
        t = (jnp.where(sg > 0.0, q, k) * jnp.exp((b - b_ref) * sg)).astype(BF16)
        scores = jnp.where(level == l, _dot_nt(t, t), scores)
    intra = _dot(scores.astype(BF16), v)
    q_dec = (q * jnp.exp(b)).astype(BF16)
    k_dec = (k * jnp.exp(b_last - b)).astype(BF16)
    update = _dot_tn(k_dec, v)
    decay_col = jnp.exp(jnp.transpose(jnp.broadcast_to(b_last, (LANE, DK))))
    return intra, q_dec, update, decay_col


def _gla_kernel(q_ref, k_ref, v_ref, r_ref, g1_ref, w2_ref, bg_ref, hg_ref, ms_ref, sgn_ref, lv_ref,
                o_ref, state_ref, *, rows):
    @pl.when(pl.program_id(2) == 0)
    def _():
        state_ref[...] = jnp.zeros_like(state_ref)

    w2 = w2_ref[...]
    b_gate = bg_ref[...]
    head_g = hg_ref[...]
    mstack = ms_ref[...]
    level = lv_ref[...]
    chunks = [pl.ds(c * GLA_CHUNK, GLA_CHUNK) for c in range(rows // GLA_CHUNK)]
    decays = [_gla_decays(g1_ref[sl, :], w2, b_gate, mstack) for sl in chunks]
    local = []
    for sl, bs in zip(chunks, decays):
        q = q_ref[sl, :].astype(F32) * (GLA_DK ** -0.5)
        local.append(_gla_local(q, k_ref[sl, :].astype(F32), v_ref[sl, :], bs, sgn_ref, level))
    state = state_ref[...]
    for sl, (intra, q_dec, update, decay_col) in zip(chunks, local):
        o = _dot(q_dec, state.astype(BF16)) + intra
        state = jnp.concatenate([decay_col] * (state.shape[1] // LANE), axis=1) * state + update
        r = r_ref[sl, :].astype(F32)
        o_ref[sl, :] = (_rmsnorm(o, head_g) * (r * _sigmoid(r))).astype(o_ref.dtype)
    state_ref[...] = state


def _gla(proj, g1, w2, b_gate, head_g, batch, seq, *, rows=512):
    T = proj.shape[0]
    H, DK, DV = GLA_HEADS, GLA_DK, GLA_DV
    rows = min(rows, seq)
    nc = seq // rows
    rowmap = lambda b, h, c: b * nc + c
    const = lambda b, h, c: (0, 0)
    v_off = 2 * H * DK // DV
    r_off = v_off + H
    mstack, sgn, level = _gla_tables()
    return pl.pallas_call(
        functools.partial(_gla_kernel, rows=rows),
        grid=(batch, H, nc),
        in_specs=[
            pl.BlockSpec((rows, DK), lambda b, h, c: (rowmap(b, h, c), h)),
            pl.BlockSpec((rows, DK), lambda b, h, c: (rowmap(b, h, c), H + h)),
            pl.BlockSpec((rows, DV), lambda b, h, c: (rowmap(b, h, c), v_off + h)),
            pl.BlockSpec((rows, DV), lambda b, h, c: (rowmap(b, h, c), r_off + h)),
            pl.BlockSpec((rows, LANE), lambda b, h, c: (rowmap(b, h, c), 0)),
            pl.BlockSpec((LANE, DK), lambda b, h, c: (0, h)),
            pl.BlockSpec((1, DK), lambda b, h, c: (0, h)),
            pl.BlockSpec((1, DV), lambda b, h, c: (0, h)),
            pl.BlockSpec(mstack.shape, const),
            pl.BlockSpec(sgn.shape, const),
            pl.BlockSpec(level.shape, const),
        ],
        out_specs=pl.BlockSpec((rows, DV), lambda b, h, c: (rowmap(b, h, c), h)),
        out_shape=jax.ShapeDtypeStruct((T, H * DV), BF16),
        scratch_shapes=[pltpu.VMEM((DK, DV), F32)],
        compiler_params=_params("parallel", "parallel", "arbitrary"),
    )(proj, proj, proj, proj, g1, w2, b_gate, head_g, mstack, sgn, level)


def _ffn_kernel(x_ref, xh_ref, g_ref, sh_ref, sc_ref, gate_ref, wv_ref, wg_ref, cwv_ref, cwg_ref,
                cbv_ref, cbg_ref, wd_ref, fg_ref, o_ref, hn_ref, acc_ref, *, tm, per_seq, final_norm):
    i = pl.program_id(0)
    j = pl.program_id(1)
    halo = BF16_ROWS
    sub = SUBLANES
    span = tm // sub
    d_model = x_ref.shape[1]

    @pl.when(j == 0)
    def _():
        g, sh, sc = g_ref[...], sh_ref[...], sc_ref[...]
        h = _normmod(x_ref[...], g, sh, sc)
        h = pltpu.einshape("svd->vsd", h.reshape(sub, span, d_model)).reshape(tm, d_model)
        hn_ref[halo:, :] = h.astype(BF16)
        prev = _normmod(xh_ref[...], g, sh, sc)
        hn_ref[:halo, :] = jnp.where(i % per_seq == 0, 0.0, prev).astype(BF16)
        acc_ref[...] = jnp.zeros_like(acc_ref)

    hn = hn_ref[...]
    sub_id = lax.broadcasted_iota(jnp.int32, (sub, wv_ref.shape[1]), 0)

    def conv(u, cw_ref, cb_ref):
        cw = cw_ref[...]
        um = u[halo:]
        tail = u[halo - sub:halo]
        first1 = pltpu.roll(jnp.where(sub_id == sub - 1, tail, um[tm - sub:]), 1, 0)
        first2 = pltpu.roll(
            jnp.where(sub_id == sub - 1, pltpu.roll(tail, 1, 0), um[tm - 2 * sub:tm - sub]), 1, 0)
        back1 = jnp.concatenate([first1, um[:tm - sub]], axis=0)
        back2 = jnp.concatenate([first2, first1, um[:tm - 2 * sub]], axis=0)
        return cb_ref[...] + cw[2:3, :] * um + cw[1:2, :] * back1 + cw[0:1, :] * back2

    val = conv(_dot(hn, wv_ref[...]), cwv_ref, cbv_ref)
    gt = conv(_dot(hn, wg_ref[...]), cwg_ref, cbg_ref)
    act = (gt * _sigmoid(gt) * val).astype(BF16)
    acc_ref[...] += _dot(act, wd_ref[...])

    @pl.when(j == pl.num_programs(1) - 1)
    def _():
        acc = pltpu.einshape("vsd->svd", acc_ref[...].reshape(span, sub, d_model)).reshape(tm, d_model)
        y = x_ref[...] + (1.0 + gate_ref[...]) * acc
        if final_norm:
            y = _rmsnorm(y, fg_ref[...])
        o_ref[...] = y


def _conv_ffn(x, g, shift, scale, gate, w_up, conv_w, conv_b, w_down, final_g, seq, layer, *,
              final_norm, tm=512, tf=512):
    assert conv_w.shape[1] == CONV_W == 3
    T, D = x.shape
    F = w_down.shape[1]
    tm = min(tm, seq)
    per_seq = seq // tm
    nf = F // tf
    halo = BF16_ROWS
    hb = tm // halo
    batch_map = lambda i, j: (i // per_seq, 0, 0)
    return pl.pallas_call(
        functools.partial(_ffn_kernel, tm=tm, per_seq=per_seq, final_norm=final_norm),
        grid=(T // tm, nf),
        in_specs=[
            pl.BlockSpec((tm, D), lambda i, j: (i, 0)),
            pl.BlockSpec((halo, D), lambda i, j: (jnp.maximum(i * hb - 1, 0), 0)),
            pl.BlockSpec((1, D), lambda i, j: (0, 0)),
            pl.BlockSpec((None, 1, D), batch_map),
            pl.BlockSpec((None, 1, D), batch_map),
            pl.BlockSpec((None, 1, D), batch_map),
            pl.BlockSpec((None, D, tf), lambda i, j: (layer, 0, j)),
            pl.BlockSpec((None, D, tf), lambda i, j: (layer, 0, nf + j)),
            pl.BlockSpec((None, CONV_W, tf), lambda i, j: (layer, 0, j)),
            pl.BlockSpec((None, CONV_W, tf), lambda i, j: (layer, 0, nf + j)),
            pl.BlockSpec((None, 1, tf), lambda i, j: (layer, 0, j)),
            pl.BlockSpec((None, 1, tf), lambda i, j: (layer, 0, nf + j)),
            pl.BlockSpec((None, tf, D), lambda i, j: (layer, j, 0)),
            pl.BlockSpec((1, D), lambda i, j: (0, 0)),
        ],
        out_specs=pl.BlockSpec((tm, D), lambda i, j: (i, 0)),
        out_shape=jax.ShapeDtypeStruct((T, D), F32),
        scratch_shapes=[pltpu.VMEM((tm + halo, D), BF16), pltpu.VMEM((tm, D), F32)],
        compiler_params=_params("parallel", "arbitrary"),
    )(x, x, g, shift, scale, gate, w_up, w_up, conv_w, conv_w, conv_b, conv_b, w_down, final_g)


def _mla_q_kernel(x_ref, g_ref, sh_ref, sc_ref, wdq_ref, qg_ref, wn_ref, wr_ref, wrs_ref,
                  cos_ref, sin_ref, qn_ref, qr_ref, *, scale):
    h = _normmod(x_ref[...], g_ref[...], sh_ref[...], sc_ref[...]).astype(BF16)
    cq = _rmsnorm(_dot(h, wdq_ref[...]), qg_ref[...]).astype(BF16)
    qn_ref[...] = (_dot(cq, wn_ref[...]) * scale).astype(BF16)
    cos = jnp.concatenate([cos_ref[...]] * MLA_HEADS, axis=1)
    sin = jnp.concatenate([sin_ref[...]] * MLA_HEADS, axis=1)
    qr = _dot(cq, wr_ref[...]) * cos + _dot(cq, wrs_ref[...]) * sin
    qr_ref[...] = (qr * scale).astype(BF16)


def _mla_q(x, g, shift, scale_mod, w_dq, q_g, w_n, w_r, w_rs, cos, sin, seq, *, tm=512):
    T, D = x.shape
    R = w_dq.shape[1]
    N = w_n.shape[1]
    tm = min(tm, seq)
    per_seq = seq // tm
    batch_map = lambda i: (i // per_seq, 0, 0)
    const = lambda i: (0, 0)
    rowblk = lambda i: (i, 0)
    return pl.pallas_call(
        functools.partial(_mla_q_kernel, scale=(MLA_NOPE + MLA_ROPE) ** -0.5),
        grid=(T // tm,),
        in_specs=[
            pl.BlockSpec((tm, D), rowblk),
            pl.BlockSpec((1, D), const),
            pl.BlockSpec((None, 1, D), batch_map),
            pl.BlockSpec((None, 1, D), batch_map),
            pl.BlockSpec((D, R), const),
            pl.BlockSpec((1, R), const),
            pl.BlockSpec((R, N), const),
            pl.BlockSpec((R, N), const),
            pl.BlockSpec((R, N), const),
            pl.BlockSpec((tm, LANE), rowblk),
            pl.BlockSpec((tm, LANE), rowblk),
        ],
        out_specs=[pl.BlockSpec((tm, N), rowblk), pl.BlockSpec((tm, N), rowblk)],
        out_shape=[jax.ShapeDtypeStruct((T, N), BF16), jax.ShapeDtypeStruct((T, N), BF16)],
        compiler_params=_params("parallel"),
    )(x, g, shift, scale_mod, w_dq, q_g, w_n, w_r, w_rs, cos, sin)


def _mla_kv_kernel(x_ref, g_ref, sh_ref, sc_ref, wdkv_ref, lg_ref, wk_ref, wv_ref,
                   cos_ref, sin_ref, kn_ref, kr_ref, v_ref):
    h = _normmod(x_ref[...], g_ref[...], sh_ref[...], sc_ref[...]).astype(BF16)
    ckv = _dot(h, wdkv_ref[...])
    lat = _rmsnorm(ckv[:, :MLA_KV_RANK], lg_ref[...]).astype(BF16)
    kn_ref[...] = _dot(lat, wk_ref[...]).astype(BF16)
    v_ref[...] = _dot(lat, wv_ref[...]).astype(BF16)
    kr = (ckv[:, MLA_KV_RANK:MLA_KV_RANK + LANE] * cos_ref[...]
          + ckv[:, MLA_KV_RANK + LANE:] * sin_ref[...])
    kr_ref[...] = kr.astype(BF16)


def _mla_kv(x, g, shift, scale_mod, w_dkv, lat_g, w_k, w_v, cos, sin, seq, *, tm=512):
    T, D = x.shape
    R = w_k.shape[0]
    N = w_k.shape[1]
    tm = min(tm, seq)
    per_seq = seq // tm
    batch_map = lambda i: (i // per_seq, 0, 0)
    const = lambda i: (0, 0)
    rowblk = lambda i: (i, 0)
    return pl.pallas_call(
        _mla_kv_kernel,
        grid=(T // tm,),
        in_specs=[
            pl.BlockSpec((tm, D), rowblk),
            pl.BlockSpec((1, D), const),
            pl.BlockSpec((None, 1, D), batch_map),
            pl.BlockSpec((None, 1, D), batch_map),
            pl.BlockSpec((D, w_dkv.shape[1]), const),
            pl.BlockSpec((1, R), const),
            pl.BlockSpec((R, N), const),
            pl.BlockSpec((R, N), const),
            pl.BlockSpec((tm, LANE), rowblk),
            pl.BlockSpec((tm, LANE), rowblk),
        ],
        out_specs=[pl.BlockSpec((tm, N), rowblk), pl.BlockSpec((tm, LANE), rowblk),
                   pl.BlockSpec((tm, N), rowblk)],
        out_shape=[jax.ShapeDtypeStruct((T, N), BF16), jax.ShapeDtypeStruct((T, LANE), BF16),
                   jax.ShapeDtypeStruct((T, N), BF16)],
        compiler_params=_params("parallel"),
    )(x, g, shift, scale_mod, w_dkv, lat_g, w_k, w_v, cos, sin)


def _attn_kernel(qn_ref, qr_ref, kn_ref, kr_ref, v_ref, o_ref, *, blk):
    seq = qn_ref.shape[0]
    ones = jnp.ones((blk, LANE), BF16)
    row = lax.broadcasted_iota(jnp.int32, (blk, blk), 0)
    col = lax.broadcasted_iota(jnp.int32, (blk, blk), 1)
    for qi in range(seq // blk):
        qrows = pl.ds(qi * blk, blk)
        q = jnp.concatenate([qn_ref[qrows, :], qr_ref[qrows, :]], axis=1)
        m = jnp.full((blk, LANE), -jnp.inf, F32)
        acc = jnp.zeros((blk, 2 * LANE), F32)
        for ki in range(qi + 1):
            krows = pl.ds(ki * blk, blk)
            k = jnp.concatenate([kn_ref[krows, :], kr_ref[krows, :]], axis=1)
            s = _dot_nt(q, k)
            if ki == qi:
                s = jnp.where(col <= row, s, -jnp.inf)
            m_new = jnp.maximum(m, jnp.max(s, axis=-1, keepdims=True))
            alpha = jnp.exp(m - m_new)
            p = jnp.exp(s - jnp.concatenate([m_new] * (blk // LANE), axis=1))
            v1 = jnp.concatenate([v_ref[krows, :], ones], axis=1)
            acc = jnp.concatenate([alpha, alpha], axis=1) * acc + _dot(p.astype(BF16), v1)
            m = m_new
        o_ref[qrows, :] = (acc[:, :LANE] / acc[:, LANE:]).astype(o_ref.dtype)


def _attention(qn, qr, kn, kr, v, batch, seq, *, blk=256):
    T = qn.shape[0]
    H = MLA_HEADS
    blk = min(blk, seq)
    head = lambda b, h: (b, h)
    return pl.pallas_call(
        functools.partial(_attn_kernel, blk=blk),
        grid=(batch, H),
        in_specs=[
            pl.BlockSpec((seq, LANE), head),
            pl.BlockSpec((seq, LANE), head),
            pl.BlockSpec((seq, LANE), head),
            pl.BlockSpec((seq, LANE), lambda b, h: (b, 0)),
            pl.BlockSpec((seq, LANE), head),
        ],
        out_specs=pl.BlockSpec((seq, LANE), head),
        out_shape=jax.ShapeDtypeStruct((T, H * MLA_V), BF16),
        compiler_params=_params("parallel", "parallel"),
    )(qn, qr, kn, kr, v)


def _pad_cols(w, n):
    return jnp.pad(w, [(0, 0)] * (w.ndim - 1) + [(0, n - w.shape[-1])])


def _rotate_half_cols(w):
    half = w.shape[-1] // 2
    return jnp.concatenate([-w[..., half:], w[..., :half]], axis=-1)


def kernel(x, c, positions, mod_w, mod_b, norm_mix_g, norm_ffn_g, gla_w_in, gla_w_gate2, gla_b_gate,
           gla_head_g, gla_w_o, kv_norm_g, kv_mod_w, kv_mod_b, mla_w_dkv, mla_kv_norm_g, mla_w_ukv,
           mla_w_dq, mla_q_norm_g, mla_w_uq, mla_w_o, ffn_w_up, ffn_conv_w, ffn_conv_b, ffn_w_down,
           final_norm_g):
    B, S, D = x.shape
    T = B * S
    depth = mod_w.shape[0]
    n_gla = gla_w_in.shape[0]
    H = MLA_HEADS

    mod = _mod_linear(c, mod_w, mod_b[:, None, :]).reshape(depth, B, 6, 1, D)
    kv_mod = _mod_linear(c, kv_mod_w[None], kv_mod_b[None, None, :]).reshape(B, 2, 1, D)

    inv = ROPE_THETA ** (-jnp.arange(0, MLA_ROPE, 2, dtype=F32) / MLA_ROPE)
    ang = positions.astype(F32)[..., None] * inv
    cos = jnp.tile(jnp.cos(ang), (1, 1, LANE // (MLA_ROPE // 2))).reshape(T, LANE)
    sin = jnp.tile(jnp.sin(ang), (1, 1, LANE // (MLA_ROPE // 2))).reshape(T, LANE)

    ffn_w_up_bf = ffn_w_up.astype(BF16)
    ffn_w_down_bf = ffn_w_down.astype(BF16)

    xs = x.reshape(T, D)
    for i in range(depth):
        sh_m, sc_m, g_m, sh_f, sc_f, g_f = (mod[i, :, t] for t in range(6))
        g_mix = norm_mix_g[i][None, :]
        if i < n_gla:
            gla_cols = 2 * GLA_HEADS * GLA_DK + 2 * GLA_HEADS * GLA_DV
            w_in = gla_w_in[i][:, :gla_cols].astype(BF16)
            w_g1 = _pad_cols(gla_w_in[i][:, gla_cols:], LANE).astype(BF16)
            proj, g1 = _normmod_matmul(xs, g_mix, sh_m, sc_m, w_in, w_g1, S, tm=1024, tn=1536, out_dtype=BF16)
            w2 = jnp.pad(gla_w_gate2[i], ((0, LANE - GLA_GATE_RANK), (0, 0))).astype(BF16)
            y = _gla(proj, g1, w2, gla_b_gate[i][None, :], gla_head_g[i].reshape(1, -1), B, S)
            w_o = gla_w_o[i].astype(BF16)
        else:
            j = i - n_gla
            if j == 0:
                w_dkv = mla_w_dkv
                w_kr = w_dkv[:, MLA_KV_RANK:]
                w_dkv_cat = jnp.concatenate(
                    [w_dkv[:, :MLA_KV_RANK], _pad_cols(w_kr, LANE), _pad_cols(_rotate_half_cols(w_kr), LANE)],
                    axis=1).astype(BF16)
                w_ukv = mla_w_ukv.reshape(MLA_KV_RANK, H, MLA_NOPE + MLA_V)
                w_k = w_ukv[:, :, :MLA_NOPE].reshape(MLA_KV_RANK, H * MLA_NOPE).astype(BF16)
                w_v = w_ukv[:, :, MLA_NOPE:].reshape(MLA_KV_RANK, H * MLA_V).astype(BF16)
                kn, kr, v = _mla_kv(xs, kv_norm_g[None, :], kv_mod[:, 0], kv_mod[:, 1], w_dkv_cat,
                                    mla_kv_norm_g[None, :], w_k, w_v, cos, sin, S)
            w_uq = mla_w_uq[j].reshape(-1, H, MLA_NOPE + MLA_ROPE)
            rq = w_uq.shape[0]
            w_n = w_uq[:, :, :MLA_NOPE].reshape(rq, H * LANE).astype(BF16)
            w_rope = w_uq[:, :, MLA_NOPE:]
            w_r = _pad_cols(w_rope, LANE).reshape(rq, H * LANE).astype(BF16)
            w_rs = _pad_cols(_rotate_half_cols(w_rope), LANE).reshape(rq, H * LANE).astype(BF16)
            qn, qr = _mla_q(xs, g_mix, sh_m, sc_m, mla_w_dq[j].astype(BF16), mla_q_norm_g[j][None, :],
                            w_n, w_r, w_rs, cos, sin, S)
            y = _attention(qn, qr, kn, kr, v, B, S)
            w_o = mla_w_o[j].astype(BF16)
        xs = _matmul_residual(y, w_o, xs, g_m, S)
        xs = _conv_ffn(xs, norm_ffn_g[i][None, :], sh_f, sc_f, g_f, ffn_w_up_bf, ffn_conv_w,
                       ffn_conv_b[:, None, :], ffn_w_down_bf, final_norm_g[None, :], S, i,
                       final_norm=(i == depth - 1))
    return xs.reshape(B, S, D)
```

```python
import functools

import jax
import jax.numpy as jnp
import numpy as np
from jax import lax
from jax.experimental import pallas as pl
from jax.experimental.pallas import tpu as pltpu

F32 = jnp.float32
BF16 = jnp.bfloat16

EPS = 1e-6
LOG2E = 1.4426950408889634
LANE = 128
SUBLANES = 8
BF16_ROWS = 16
VMEM_LIMIT_BYTES = 56 << 20

GLA_HEADS = 4
GLA_DK = 256
GLA_DV = 512
GLA_GATE_RANK = 16
GLA_GATE_NORM = 16.0
GLA_CHUNK = 64
GLA_LEVELS = 6
GLA_MIN_BCAST = 4

MLA_HEADS = 16
MLA_NOPE = 128
MLA_ROPE = 64
MLA_V = 128
MLA_KV_RANK = 512
ROPE_THETA = 10000.0

CONV_W = 3


def _params(*sem):
    return pltpu.CompilerParams(dimension_semantics=sem, vmem_limit_bytes=VMEM_LIMIT_BYTES)


def _dot(a, b):
    return jnp.dot(a, b, preferred_element_type=F32)


def _dot_nt(a, b):
    return lax.dot_general(a, b, (((1,), (1,)), ((), ())), preferred_element_type=F32)


def _dot_tn(a, b):
    return lax.dot_general(a, b, (((0,), (0,)), ((), ())), preferred_element_type=F32)


def _sigmoid(x):
    return 1.0 / (1.0 + jnp.exp(-x))


def _rmsnorm(x, g):
    return x * lax.rsqrt(jnp.mean(x * x, axis=-1, keepdims=True) + EPS) * g


def _normmod(x, g, shift, scale):
    return _rmsnorm(x, g) * (1.0 + scale) + shift


def _mod_kernel(c_ref, w_ref, b_ref, o_ref):
    c = c_ref[...]
    c_act = (c * _sigmoid(c)).astype(BF16)
    o_ref[...] = _dot(c_act, w_ref[...].astype(BF16)) + b_ref[...]


def _mod_linear(c, w, b, *, tn=1024):
    L, D, N = w.shape
    B = c.shape[0]
    return pl.pallas_call(
        _mod_kernel,
        grid=(L, N // tn),
        in_specs=[
            pl.BlockSpec((B, D), lambda l, j: (0, 0)),
            pl.BlockSpec((None, D, tn), lambda l, j: (l, 0, j)),
            pl.BlockSpec((None, 1, tn), lambda l, j: (l, 0, j)),
        ],
        out_specs=pl.BlockSpec((None, B, tn), lambda l, j: (l, 0, j)),
        out_shape=jax.ShapeDtypeStruct((L, B, N), F32),
        compiler_params=_params("parallel", "parallel"),
    )(c, w, b)


def _normmod_matmul_kernel(x_ref, g_ref, sh_ref, sc_ref, w_ref, ws_ref, o_ref, os_ref, hn_ref):
    @pl.when(pl.program_id(1) == 0)
    def _():
        hn_ref[...] = _normmod(x_ref[...], g_ref[...], sh_ref[...], sc_ref[...]).astype(BF16)
        os_ref[...] = _dot(hn_ref[...], ws_ref[...]).astype(os_ref.dtype)

    o_ref[...] = _dot(hn_ref[...], w_ref[...]).astype(o_ref.dtype)


def _normmod_matmul(x, g, shift, scale, w, w_side, seq, *, tm, tn, out_dtype):
    T, D = x.shape
    N = w.shape[1]
    tm = min(tm, seq)
    per_seq = seq // tm
    return pl.pallas_call(
        _normmod_matmul_kernel,
        grid=(T // tm, N // tn),
        in_specs=[
            pl.BlockSpec((tm, D), lambda i, j: (i, 0)),
            pl.BlockSpec((1, D), lambda i, j: (0, 0)),
            pl.BlockSpec((None, 1, D), lambda i, j: (i // per_seq, 0, 0)),
            pl.BlockSpec((None, 1, D), lambda i, j: (i // per_seq, 0, 0)),
            pl.BlockSpec((D, tn), lambda i, j: (0, j)),
            pl.BlockSpec((D, LANE), lambda i, j: (0, 0)),
        ],
        out_specs=[pl.BlockSpec((tm, tn), lambda i, j: (i, j)),
                   pl.BlockSpec((tm, LANE), lambda i, j: (i, 0))],
        out_shape=[jax.ShapeDtypeStruct((T, N), out_dtype), jax.ShapeDtypeStruct((T, LANE), out_dtype)],
        scratch_shapes=[pltpu.VMEM((tm, D), BF16)],
        compiler_params=_params("parallel", "arbitrary"),
    )(x, g, shift, scale, w, w_side)


def _matmul_residual_kernel(a_ref, w_ref, res_ref, gate_ref, o_ref):
    o_ref[...] = res_ref[...] + (1.0 + gate_ref[...]) * _dot(a_ref[...], w_ref[...])


def _matmul_residual(a, w, res, gate, seq, *, tm=512):
    T, K = a.shape
    N = w.shape[1]
    tn = N
    tm = min(tm, seq)
    per_seq = seq // tm
    return pl.pallas_call(
        _matmul_residual_kernel,
        grid=(T // tm, N // tn),
        in_specs=[
            pl.BlockSpec((tm, K), lambda i, j: (i, 0)),
            pl.BlockSpec((K, tn), lambda i, j: (0, j)),
            pl.BlockSpec((tm, tn), lambda i, j: (i, j)),
            pl.BlockSpec((None, 1, tn), lambda i, j: (i // per_seq, 0, j)),
        ],
        out_specs=pl.BlockSpec((tm, tn), lambda i, j: (i, j)),
        out_shape=jax.ShapeDtypeStruct((T, N), F32),
        compiler_params=_params("parallel", "parallel"),
    )(a, w, res, gate)


def _gla_tables():
    C = GLA_CHUNK
    r = np.arange(C)
    tri = (r[None, :] <= r[:, None]).astype(np.float32)
    mats, sgn = [tri], []
    level = np.full((C, C), -1, np.int32)
    level[r, r] = GLA_LEVELS
    for l in range(GLA_LEVELS):
        h = C >> (l + 1)
        if h < GLA_MIN_BCAST:
            mats.append(tri[(r // (2 * h)) * (2 * h) + h])
        is_q = (r // h) % 2 == 1
        sgn.append(np.where(is_q, 1.0, -1.0))
        level[is_q[:, None] & ((r[None, :] // h) == (r[:, None] // h) - 1)] = l
    mstack = jnp.asarray(np.concatenate(mats, axis=0), BF16)
    sgn = jnp.asarray(np.repeat(np.concatenate(sgn)[:, None], GLA_DK, axis=1), F32)
    return mstack, sgn, jnp.asarray(level)


def _gla_decays(g1, w2, b_gate, mstack):
    z = _dot(g1, w2) + b_gate
    log_a = (jnp.minimum(z, 0.0) - jnp.log(1.0 + jnp.exp(-jnp.abs(z)))) * (1.0 / GLA_GATE_NORM)
    la_hi = log_a.astype(BF16)
    la_lo = (log_a - la_hi.astype(F32)).astype(BF16)
    return _dot(mstack, la_hi) + _dot(mstack, la_lo)


def _gla_local(q, k, v, bs, sgn_ref, level):
    C, DK = q.shape
    b = bs[:C]
    b_last = b[C - 1:C, :]
    n_fine = bs.shape[0] // C - 1
    scores = jnp.where(level == GLA_LEVELS, jnp.sum(q * k, axis=-1, keepdims=True), 0.0)
    for l in range(GLA_LEVELS):
        sg = sgn_ref[l * C:(l + 1) * C, :]
        h = C >> (l + 1)
        if h >= GLA_MIN_BCAST:
            b_ref = jnp.concatenate(
                [jnp.broadcast_to(b[r0 + h:r0 + h + 1, :], (2 * h, DK)) for r0 in range(0, C, 2 * h)], axis=0)
        else:
            fine = l - (GLA_LEVELS - n_fine)
            b_ref = bs[(fine + 1) * C:(fine + 2) * C]
        t = (jnp.where(sg > 0.0, q, k) * jnp.exp((b - b_ref) * sg)).astype(BF16)
        scores = jnp.where(level == l, _dot_nt(t, t), scores)
    intra = _dot(scores.astype(BF16), v)
    q_dec = (q * jnp.exp(b)).astype(BF16)
    k_dec = (k * jnp.exp(b_last - b)).astype(BF16)
    update = _dot_tn(k_dec, v)
    decay_col = jnp.exp(jnp.transpose(jnp.broadcast_to(b_last, (LANE, DK))))
    return intra, q_dec, update, decay_col


def _gla_kernel(q_ref, k_ref, v_ref, r_ref, g1_ref, w2_ref, bg_ref, hg_ref, ms_ref, sgn_ref, lv_ref,
                o_ref, state_ref, *, rows):
    @pl.when(pl.program_id(2) == 0)
    def _():
        state_ref[...] = jnp.zeros_like(state_ref)

    w2 = w2_ref[...]
    b_gate = bg_ref[...]
    head_g = hg_ref[...]
    mstack = ms_ref[...]
    level = lv_ref[...]
    chunks = [pl.ds(c * GLA_CHUNK, GLA_CHUNK) for c in range(rows // GLA_CHUNK)]
    decays = [_gla_decays(g1_ref[sl, :], w2, b_gate, mstack) for sl in chunks]
    local = []
    for sl, bs in zip(chunks, decays):
        q = q_ref[sl, :].astype(F32) * (GLA_DK ** -0.5)
        local.append(_gla_local(q, k_ref[sl, :].astype(F32), v_ref[sl, :], bs, sgn_ref, level))
    state = state_ref[...]
    for sl, (intra, q_dec, update, decay_col) in zip(chunks, local):
        o = _dot(q_dec, state.astype(BF16)) + intra
        state = jnp.concatenate([decay_col] * (state.shape[1] // LANE), axis=1) * state + update
        r = r_ref[sl, :].astype(F32)
        o_ref[sl, :] = (_rmsnorm(o, head_g) * (r * _sigmoid(r))).astype(o_ref.dtype)
    state_ref[...] = state


def _gla(proj, g1, w2, b_gate, head_g, batch, seq, *, rows=512):
    T = proj.shape[0]
    H, DK, DV = GLA_HEADS, GLA_DK, GLA_DV
    rows = min(rows, seq)
    nc = seq // rows
    rowmap = lambda b, h, c: b * nc + c
    const = lambda b, h, c: (0, 0)
    v_off = 2 * H * DK // DV
    r_off = v_off + H
    mstack, sgn, level = _gla_tables()
    return pl.pallas_call(
        functools.partial(_gla_kernel, rows=rows),
        grid=(batch, H, nc),
        in_specs=[
            pl.BlockSpec((rows, DK), lambda b, h, c: (rowmap(b, h, c), h)),
            pl.BlockSpec((rows, DK), lambda b, h, c: (rowmap(b, h, c), H + h)),
            pl.BlockSpec((rows, DV), lambda b, h, c: (rowmap(b, h, c), v_off + h)),
            pl.BlockSpec((rows, DV), lambda b, h, c: (rowmap(b, h, c), r_off + h)),
            pl.BlockSpec((rows, LANE), lambda b, h, c: (rowmap(b, h, c), 0)),
            pl.BlockSpec((LANE, DK), lambda b, h, c: (0, h)),
            pl.BlockSpec((1, DK), lambda b, h, c: (0, h)),
            pl.BlockSpec((1, DV), lambda b, h, c: (0, h)),
            pl.BlockSpec(mstack.shape, const),
            pl.BlockSpec(sgn.shape, const),
            pl.BlockSpec(level.shape, const),
        ],
        out_specs=pl.BlockSpec((rows, DV), lambda b, h, c: (rowmap(b, h, c), h)),
        out_shape=jax.ShapeDtypeStruct((T, H * DV), BF16),
        scratch_shapes=[pltpu.VMEM((DK, DV), F32)],
        compiler_params=_params("parallel", "parallel", "arbitrary"),
    )(proj, proj, proj, proj, g1, w2, b_gate, head_g, mstack, sgn, level)


def _ffn_kernel(x_ref, xh_ref, g_ref, sh_ref, sc_ref, gate_ref, wv_ref, wg_ref, cwv_ref, cwg_ref,
                cbv_ref, cbg_ref, wd_ref, fg_ref, o_ref, hn_ref, acc_ref, *, tm, per_seq, final_norm):
    i = pl.program_id(0)
    j = pl.program_id(1)
    halo = BF16_ROWS
    sub = SUBLANES
    span = tm // sub
    d_model = x_ref.shape[1]

    @pl.when(j == 0)
    def _():
        g, sh, sc = g_ref[...], sh_ref[...], sc_ref[...]
        h = _normmod(x_ref[...], g, sh, sc)
        h = pltpu.einshape("svd->vsd", h.reshape(sub, span, d_model)).reshape(tm, d_model)
        hn_ref[halo:, :] = h.astype(BF16)
        prev = _normmod(xh_ref[...], g, sh, sc)
        hn_ref[:halo, :] = jnp.where(i % per_seq == 0, 0.0, prev).astype(BF16)
        acc_ref[...] = jnp.zeros_like(acc_ref)

    hn = hn_ref[...]
    sub_id = lax.broadcasted_iota(jnp.int32, (sub, wv_ref.shape[1]), 0)

    def conv(u, cw_ref, cb_ref):
        cw = cw_ref[...]
        um = u[halo:]
        tail = u[halo - sub:halo]
        first1 = pltpu.roll(jnp.where(sub_id == sub - 1, tail, um[tm - sub:]), 1, 0)
        first2 = pltpu.roll(
            jnp.where(sub_id == sub - 1, pltpu.roll(tail, 1, 0), um[tm - 2 * sub:tm - sub]), 1, 0)
        back1 = jnp.concatenate([first1, um[:tm - sub]], axis=0)
        back2 = jnp.concatenate([first2, first1, um[:tm - 2 * sub]], axis=0)
        return cb_ref[...] + cw[2:3, :] * um + cw[1:2, :] * back1 + cw[0:1, :] * back2

    gt = conv(_dot(hn, wg_ref[...]), cwg_ref, cbg_ref)
    silu = gt * _sigmoid(gt)
    val = conv(_dot(hn, wv_ref[...]), cwv_ref, cbv_ref)
    act = (silu * val).astype(BF16)
    acc_ref[...] += _dot(act, wd_ref[...])

    @pl.when(j == pl.num_programs(1) - 1)
    def _():
        acc = pltpu.einshape("vsd->svd", acc_ref[...].reshape(span, sub, d_model)).reshape(tm, d_model)
        y = x_ref[...] + (1.0 + gate_ref[...]) * acc
        if final_norm:
            y = _rmsnorm(y, fg_ref[...])
        o_ref[...] = y


def _conv_ffn(x, g, shift, scale, gate, w_up, conv_w, conv_b, w_down, final_g, seq, layer, *,
              final_norm, tm=512, tf=512):
    assert conv_w.shape[1] == CONV_W == 3
    T, D = x.shape
    F = w_down.shape[1]
    tm = min(tm, seq)
    per_seq = seq // tm
    nf = F // tf
    halo = BF16_ROWS
    hb = tm // halo
    batch_map = lambda i, j: (i // per_seq, 0, 0)
    return pl.pallas_call(
        functools.partial(_ffn_kernel, tm=tm, per_seq=per_seq, final_norm=final_norm),
        grid=(T // tm, nf),
        in_specs=[
            pl.BlockSpec((tm, D), lambda i, j: (i, 0)),
            pl.BlockSpec((halo, D), lambda i, j: (jnp.maximum(i * hb - 1, 0), 0)),
            pl.BlockSpec((1, D), lambda i, j: (0, 0)),
            pl.BlockSpec((None, 1, D), batch_map),
            pl.BlockSpec((None, 1, D), batch_map),
            pl.BlockSpec((None, 1, D), batch_map),
            pl.BlockSpec((None, D, tf), lambda i, j: (layer, 0, j)),
            pl.BlockSpec((None, D, tf), lambda i, j: (layer, 0, nf + j)),
            pl.BlockSpec((None, CONV_W, tf), lambda i, j: (layer, 0, j)),
            pl.BlockSpec((None, CONV_W, tf), lambda i, j: (layer, 0, nf + j)),
            pl.BlockSpec((None, 1, tf), lambda i, j: (layer, 0, j)),
            pl.BlockSpec((None, 1, tf), lambda i, j: (layer, 0, nf + j)),
            pl.BlockSpec((None, tf, D), lambda i, j: (layer, j, 0)),
            pl.BlockSpec((1, D), lambda i, j: (0, 0)),
        ],
        out_specs=pl.BlockSpec((tm, D), lambda i, j: (i, 0)),
        out_shape=jax.ShapeDtypeStruct((T, D), F32),
        scratch_shapes=[pltpu.VMEM((tm + halo, D), BF16), pltpu.VMEM((tm, D), F32)],
        compiler_params=_params("parallel", "arbitrary"),
    )(x, x, g, shift, scale, gate, w_up, w_up, conv_w, conv_w, conv_b, conv_b, w_down, final_g)


def _mla_q_kernel(x_ref, g_ref, sh_ref, sc_ref, wdq_ref, qg_ref, wn_ref, wr_ref, wrs_ref,
                  cos_ref, sin_ref, qn_ref, qr_ref, *, scale):
    h = _normmod(x_ref[...], g_ref[...], sh_ref[...], sc_ref[...]).astype(BF16)
    cq = _rmsnorm(_dot(h, wdq_ref[...]), qg_ref[...]).astype(BF16)
    qn_ref[...] = (_dot(cq, wn_ref[...]) * scale).astype(BF16)
    n_rep = wr_ref.shape[1] // LANE
    cos = jnp.concatenate([cos_ref[...]] * n_rep, axis=1)
    sin = jnp.concatenate([sin_ref[...]] * n_rep, axis=1)
    qr = _dot(cq, wr_ref[...]) * cos + _dot(cq, wrs_ref[...]) * sin
    qr_ref[...] = (qr * scale).astype(BF16)


def _mla_q(x, g, shift, scale_mod, w_dq, q_g, w_n, w_r, w_rs, cos, sin, seq, *, tm=512):
    T, D = x.shape
    R = w_dq.shape[1]
    N = w_n.shape[1]
    NR = w_r.shape[1]
    tm = min(tm, seq)
    per_seq = seq // tm
    batch_map = lambda i: (i // per_seq, 0, 0)
    const = lambda i: (0, 0)
    rowblk = lambda i: (i, 0)
    return pl.pallas_call(
        functools.partial(_mla_q_kernel, scale=LOG2E * (MLA_NOPE + MLA_ROPE) ** -0.5),
        grid=(T // tm,),
        in_specs=[
            pl.BlockSpec((tm, D), rowblk),
            pl.BlockSpec((1, D), const),
            pl.BlockSpec((None, 1, D), batch_map),
            pl.BlockSpec((None, 1, D), batch_map),
            pl.BlockSpec((D, R), const),
            pl.BlockSpec((1, R), const),
            pl.BlockSpec((R, N), const),
            pl.BlockSpec((R, NR), const),
            pl.BlockSpec((R, NR), const),
            pl.BlockSpec((tm, LANE), rowblk),
            pl.BlockSpec((tm, LANE), rowblk),
        ],
        out_specs=[pl.BlockSpec((tm, N), rowblk), pl.BlockSpec((tm, NR), rowblk)],
        out_shape=[jax.ShapeDtypeStruct((T, N), BF16), jax.ShapeDtypeStruct((T, NR), BF16)],
        compiler_params=_params("parallel"),
    )(x, g, shift, scale_mod, w_dq, q_g, w_n, w_r, w_rs, cos, sin)


def _mla_kv_kernel(x_ref, g_ref, sh_ref, sc_ref, wdkv_ref, lg_ref, wk_ref, wv_ref,
                   cos_ref, sin_ref, kn_ref, kr_ref, v_ref):
    h = _normmod(x_ref[...], g_ref[...], sh_ref[...], sc_ref[...]).astype(BF16)
    ckv = _dot(h, wdkv_ref[...])
    lat = _rmsnorm(ckv[:, :MLA_KV_RANK], lg_ref[...]).astype(BF16)
    kn_ref[...] = _dot(lat, wk_ref[...]).astype(BF16)
    v_ref[...] = _dot(lat, wv_ref[...]).astype(BF16)
    kr = (ckv[:, MLA_KV_RANK:MLA_KV_RANK + LANE] * cos_ref[...]
          + ckv[:, MLA_KV_RANK + LANE:] * sin_ref[...])
    kr_ref[...] = jnp.concatenate([kr, pltpu.roll(kr, LANE // 2, 1)], axis=1).astype(BF16)


def _mla_kv(x, g, shift, scale_mod, w_dkv, lat_g, w_k, w_v, cos, sin, seq, *, tm=512):
    T, D = x.shape
    R = w_k.shape[0]
    N = w_k.shape[1]
    tm = min(tm, seq)
    per_seq = seq // tm
    batch_map = lambda i: (i // per_seq, 0, 0)
    const = lambda i: (0, 0)
    rowblk = lambda i: (i, 0)
    return pl.pallas_call(
        _mla_kv_kernel,
        grid=(T // tm,),
        in_specs=[
            pl.BlockSpec((tm, D), rowblk),
            pl.BlockSpec((1, D), const),
            pl.BlockSpec((None, 1, D), batch_map),
            pl.BlockSpec((None, 1, D), batch_map),
            pl.BlockSpec((D, w_dkv.shape[1]), const),
            pl.BlockSpec((1, R), const),
            pl.BlockSpec((R, N), const),
            pl.BlockSpec((R, N), const),
            pl.BlockSpec((tm, LANE), rowblk),
            pl.BlockSpec((tm, LANE), rowblk),
        ],
        out_specs=[pl.BlockSpec((tm, N), rowblk), pl.BlockSpec((tm, 2 * LANE), rowblk),
                   pl.BlockSpec((tm, N), rowblk)],
        out_shape=[jax.ShapeDtypeStruct((T, N), BF16), jax.ShapeDtypeStruct((T, 2 * LANE), BF16),
                   jax.ShapeDtypeStruct((T, N), BF16)],
        compiler_params=_params("parallel"),
    )(x, g, shift, scale_mod, w_dkv, lat_g, w_k, w_v, cos, sin)


def _attn_kernel(qn_ref, qr_ref, kn_ref, kr_ref, v_ref, o_ref, *, blk):
    seq = qn_ref.shape[0]
    ones = jnp.ones((blk, LANE), BF16)
    row = lax.broadcasted_iota(jnp.int32, (blk, blk), 0)
    col = lax.broadcasted_iota(jnp.int32, (blk, blk), 1)
    for qi in range(seq // blk):
        qrows = pl.ds(qi * blk, blk)
        q = jnp.concatenate([qn_ref[qrows, :], qr_ref[qrows, :]], axis=1)
        m = jnp.full((blk, LANE), -jnp.inf, F32)
        acc = jnp.zeros((blk, 2 * LANE), F32)
        for ki in range(qi + 1):
            krows = pl.ds(ki * blk, blk)
            k = jnp.concatenate([kn_ref[krows, :], kr_ref[krows, :]], axis=1)
            s = _dot_nt(q, k)
            if ki == qi:
                s = jnp.where(col <= row, s, -jnp.inf)
            m_new = jnp.maximum(m, jnp.max(s, axis=-1, keepdims=True))
            alpha = jnp.exp2(m - m_new)
            p = jnp.exp2(s - jnp.concatenate([m_new] * (blk // LANE), axis=1))
            v1 = jnp.concatenate([v_ref[krows, :], ones], axis=1)
            acc = jnp.concatenate([alpha, alpha], axis=1) * acc + _dot(p.astype(BF16), v1)
            m = m_new
        o_ref[qrows, :] = (acc[:, :LANE] / acc[:, LANE:]).astype(o_ref.dtype)


def _attention(qn, qr, kn, kr, v, batch, seq, *, blk=256):
    T = qn.shape[0]
    H = MLA_HEADS
    blk = min(blk, seq)
    head = lambda b, h: (b, h)
    return pl.pallas_call(
        functools.partial(_attn_kernel, blk=blk),
        grid=(batch, H),
        in_specs=[
            pl.BlockSpec((seq, LANE), head),
            pl.BlockSpec((seq, LANE), lambda b, h: (b, h // 2)),
            pl.BlockSpec((seq, LANE), head),
            pl.BlockSpec((seq, LANE), lambda b, h: (b, h % 2)),
            pl.BlockSpec((seq, LANE), head),
        ],
        out_specs=pl.BlockSpec((seq, LANE), head),
        out_shape=jax.ShapeDtypeStruct((T, H * MLA_V), BF16),
        compiler_params=_params("parallel", "parallel"),
    )(qn, qr, kn, kr, v)


def _pad_cols(w, n):
    return jnp.pad(w, [(0, 0)] * (w.ndim - 1) + [(0, n - w.shape[-1])])


def _rotate_half_cols(w):
    half = w.shape[-1] // 2
    return jnp.concatenate([-w[..., half:], w[..., :half]], axis=-1)


def kernel(x, c, positions, mod_w, mod_b, norm_mix_g, norm_ffn_g, gla_w_in, gla_w_gate2, gla_b_gate,
           gla_head_g, gla_w_o, kv_norm_g, kv_mod_w, kv_mod_b, mla_w_dkv, mla_kv_norm_g, mla_w_ukv,
           mla_w_dq, mla_q_norm_g, mla_w_uq, mla_w_o, ffn_w_up, ffn_conv_w, ffn_conv_b, ffn_w_down,
           final_norm_g):
    B, S, D = x.shape
    T = B * S
    depth = mod_w.shape[0]
    n_gla = gla_w_in.shape[0]
    H = MLA_HEADS

    mod = _mod_linear(c, mod_w, mod_b[:, None, :]).reshape(depth, B, 6, 1, D)
    kv_mod = _mod_linear(c, kv_mod_w[None], kv_mod_b[None, None, :]).reshape(B, 2, 1, D)

    inv = ROPE_THETA ** (-jnp.arange(0, MLA_ROPE, 2, dtype=F32) / MLA_ROPE)
    ang = positions.astype(F32)[..., None] * inv
    cos = jnp.tile(jnp.cos(ang), (1, 1, LANE // (MLA_ROPE // 2))).reshape(T, LANE)
    sin = jnp.tile(jnp.sin(ang), (1, 1, LANE // (MLA_ROPE // 2))).reshape(T, LANE)

    ffn_w_up_bf = ffn_w_up.astype(BF16)
    ffn_w_down_bf = ffn_w_down.astype(BF16)

    xs = x.reshape(T, D)
    for i in range(depth):
        sh_m, sc_m, g_m, sh_f, sc_f, g_f = (mod[i, :, t] for t in range(6))
        g_mix = norm_mix_g[i][None, :]
        if i < n_gla:
            gla_cols = 2 * GLA_HEADS * GLA_DK + 2 * GLA_HEADS * GLA_DV
            w_in = gla_w_in[i][:, :gla_cols].astype(BF16)
            w_g1 = _pad_cols(gla_w_in[i][:, gla_cols:], LANE).astype(BF16)
            proj, g1 = _normmod_matmul(xs, g_mix, sh_m, sc_m, w_in, w_g1, S, tm=1024, tn=1536, out_dtype=BF16)
            w2 = jnp.pad(gla_w_gate2[i], ((0, LANE - GLA_GATE_RANK), (0, 0))).astype(BF16)
            y = _gla(proj, g1, w2, gla_b_gate[i][None, :], gla_head_g[i].reshape(1, -1), B, S)
            w_o = gla_w_o[i].astype(BF16)
        else:
            j = i - n_gla
            if j == 0:
                w_dkv = mla_w_dkv
                w_kr = w_dkv[:, MLA_KV_RANK:]
                w_dkv_cat = jnp.concatenate(
                    [w_dkv[:, :MLA_KV_RANK], _pad_cols(w_kr, LANE), _pad_cols(_rotate_half_cols(w_kr), LANE)],
                    axis=1).astype(BF16)
                w_ukv = mla_w_ukv.reshape(MLA_KV_RANK, H, MLA_NOPE + MLA_V)
                w_k = w_ukv[:, :, :MLA_NOPE].reshape(MLA_KV_RANK, H * MLA_NOPE).astype(BF16)
                w_v = w_ukv[:, :, MLA_NOPE:].reshape(MLA_KV_RANK, H * MLA_V).astype(BF16)
                kn, kr, v = _mla_kv(xs, kv_norm_g[None, :], kv_mod[:, 0], kv_mod[:, 1], w_dkv_cat,
                                    mla_kv_norm_g[None, :], w_k, w_v, cos, sin, S)
            w_uq = mla_w_uq[j].reshape(-1, H, MLA_NOPE + MLA_ROPE)
            rq = w_uq.shape[0]
            w_n = w_uq[:, :, :MLA_NOPE].reshape(rq, H * LANE).astype(BF16)
            w_rope = w_uq[:, :, MLA_NOPE:]
            w_r = w_rope.reshape(rq, H * MLA_ROPE).astype(BF16)
            w_rs = _rotate_half_cols(w_rope).reshape(rq, H * MLA_ROPE).astype(BF16)
            qn, qr = _mla_q(xs, g_mix, sh_m, sc_m, mla_w_dq[j].astype(BF16), mla_q_norm_g[j][None, :],
                            w_n, w_r, w_rs, cos, sin, S)
            y = _attention(qn, qr, kn, kr, v, B, S)
            w_o = mla_w_o[j].astype(BF16)
        xs = _matmul_residual(y, w_o, xs, g_m, S)
        xs = _conv_ffn(xs, norm_ffn_g[i][None, :], sh_f, sc_f, g_f, ffn_w_up_bf, ffn_conv_w,
                       ffn_conv_b[:, None, :], ffn_w_down_bf, final_norm_g[None, :], S, i,
                       final_norm=(i == depth - 1))
    return xs.reshape(B, S, D)
```

```python
import functools

import jax
import jax.numpy as jnp
import numpy as np
from jax import lax
from jax.experimental import pallas as pl
from jax.experimental.pallas import tpu as pltpu

F32 = jnp.float32
BF16 = jnp.bfloat16

EPS = 1e-6
LOG2E = 1.4426950408889634
LANE = 128
SUBLANES = 8
BF16_ROWS = 16
VMEM_LIMIT_BYTES = 56 << 20

GLA_HEADS = 4
GLA_DK = 256
GLA_DV = 512
GLA_GATE_RANK = 16
GLA_GATE_NORM = 16.0
GLA_CHUNK = 64
GLA_LEVELS = 6
GLA_MIN_BCAST = 4

MLA_HEADS = 16
MLA_NOPE = 128
MLA_ROPE = 64
MLA_V = 128
MLA_KV_RANK = 512
ROPE_THETA = 10000.0

CONV_W = 3


def _params(*sem):
    return pltpu.CompilerParams(dimension_semantics=sem, vmem_limit_bytes=VMEM_LIMIT_BYTES)


def _dot(a, b):
    return jnp.dot(a, b, preferred_element_type=F32)


def _dot_nt(a, b):
    return lax.dot_general(a, b, (((1,), (1,)), ((), ())), preferred_element_type=F32)


def _dot_tn(a, b):
    return lax.dot_general(a, b, (((0,), (0,)), ((), ())), preferred_element_type=F32)


def _sigmoid(x):
    return 1.0 / (1.0 + jnp.exp(-x))


def _rmsnorm(x, g):
    return x * lax.rsqrt(jnp.mean(x * x, axis=-1, keepdims=True) + EPS) * g


def _normmod(x, g, shift, scale):
    return _rmsnorm(x, g) * (1.0 + scale) + shift


def _mod_kernel(c_ref, w_ref, b_ref, o_ref):
    c = c_ref[...]
    c_act = (c * _sigmoid(c)).astype(BF16)
    o_ref[...] = _dot(c_act, w_ref[...].astype(BF16)) + b_ref[...]


def _mod_linear(c, w, b, *, tn=1024):
    L, D, N = w.shape
    B = c.shape[0]
    return pl.pallas_call(
        _mod_kernel,
        grid=(L, N // tn),
        in_specs=[
            pl.BlockSpec((B, D), lambda l, j: (0, 0)),
            pl.BlockSpec((None, D, tn), lambda l, j: (l, 0, j)),
            pl.BlockSpec((None, 1, tn), lambda l, j: (l, 0, j)),
        ],
        out_specs=pl.BlockSpec((None, B, tn), lambda l, j: (l, 0, j)),
        out_shape=jax.ShapeDtypeStruct((L, B, N), F32),
        compiler_params=_params("parallel", "parallel"),
    )(c, w, b)


def _normmod_matmul_kernel(x_ref, g_ref, sh_ref, sc_ref, w_ref, ws_ref, o_ref, os_ref, hn_ref):
    @pl.when(pl.program_id(1) == 0)
    def _():
        hn_ref[...] = _normmod(x_ref[...], g_ref[...], sh_ref[...], sc_ref[...]).astype(BF16)
        os_ref[...] = _dot(hn_ref[...], ws_ref[...]).astype(os_ref.dtype)

    o_ref[...] = _dot(hn_ref[...], w_ref[...]).astype(o_ref.dtype)


def _normmod_matmul(x, g, shift, scale, w, w_side, seq, *, tm, tn, out_dtype):
    T, D = x.shape
    N = w.shape[1]
    tm = min(tm, seq)
    per_seq = seq // tm
    return pl.pallas_call(
        _normmod_matmul_kernel,
        grid=(T // tm, N // tn),
        in_specs=[
            pl.BlockSpec((tm, D), lambda i, j: (i, 0)),
            pl.BlockSpec((1, D), lambda i, j: (0, 0)),
            pl.BlockSpec((None, 1, D), lambda i, j: (i // per_seq, 0, 0)),
            pl.BlockSpec((None, 1, D), lambda i, j: (i // per_seq, 0, 0)),
            pl.BlockSpec((D, tn), lambda i, j: (0, j)),
            pl.BlockSpec((D, LANE), lambda i, j: (0, 0)),
        ],
        out_specs=[pl.BlockSpec((tm, tn), lambda i, j: (i, j)),
                   pl.BlockSpec((tm, LANE), lambda i, j: (i, 0))],
        out_shape=[jax.ShapeDtypeStruct((T, N), out_dtype), jax.ShapeDtypeStruct((T, LANE), out_dtype)],
        scratch_shapes=[pltpu.VMEM((tm, D), BF16)],
        compiler_params=_params("parallel", "arbitrary"),
    )(x, g, shift, scale, w, w_side)


def _matmul_residual_kernel(a_ref, w_ref, res_ref, gate_ref, o_ref):
    o_ref[...] = res_ref[...] + (1.0 + gate_ref[...]) * _dot(a_ref[...], w_ref[...])


def _matmul_residual(a, w, res, gate, seq, *, tm=512):
    T, K = a.shape
    N = w.shape[1]
    tn = N
    tm = min(tm, seq)
    per_seq = seq // tm
    return pl.pallas_call(
        _matmul_residual_kernel,
        grid=(T // tm, N // tn),
        in_specs=[
            pl.BlockSpec((tm, K), lambda i, j: (i, 0)),
            pl.BlockSpec((K, tn), lambda i, j: (0, j)),
            pl.BlockSpec((tm, tn), lambda i, j: (i, j)),
            pl.BlockSpec((None, 1, tn), lambda i, j: (i // per_seq, 0, j)),
        ],
        out_specs=pl.BlockSpec((tm, tn), lambda i, j: (i, j)),
        out_shape=jax.ShapeDtypeStruct((T, N), F32),
        compiler_params=_params("parallel", "parallel"),
    )(a, w, res, gate)


def _gla_tables():
    C = GLA_CHUNK
    r = np.arange(C)
    tri = (r[None, :] <= r[:, None]).astype(np.float32)
    mats, sgn = [tri], []
    level = np.full((C, C), -1, np.int32)
    level[r, r] = GLA_LEVELS
    for l in range(GLA_LEVELS):
        h = C >> (l + 1)
        if h < GLA_MIN_BCAST:
            mats.append(tri[(r // (2 * h)) * (2 * h) + h])
        is_q = (r // h) % 2 == 1
        sgn.append(np.where(is_q, 1.0, -1.0))
        level[is_q[:, None] & ((r[None, :] // h) == (r[:, None] // h) - 1)] = l
    mstack = jnp.asarray(np.concatenate(mats, axis=0), BF16)
    sgn = jnp.asarray(np.repeat(np.concatenate(sgn)[:, None], GLA_DK, axis=1), F32)
    return mstack, sgn, jnp.asarray(level)


def _gla_decays(g1, w2, b_gate, mstack):
    z = _dot(g1, w2) + b_gate
    log_a = (jnp.minimum(z, 0.0) - jnp.log(1.0 + jnp.exp(-jnp.abs(z)))) * (1.0 / GLA_GATE_NORM)
    la_hi = log_a.astype(BF16)
    la_lo = (log_a - la_hi.astype(F32)).astype(BF16)
    return _dot(mstack, la_hi) + _dot(mstack, la_lo)


def _gla_local(q, k, v, bs, sgn_ref, level):
    C, DK = q.shape
    b = bs[:C]
    b_last = b[C - 1:C, :]
    n_fine = bs.shape[0] // C - 1
    scores = jnp.where(level == GLA_LEVELS, jnp.sum(q * k, axis=-1, keepdims=True), 0.0)
    for l in range(GLA_LEVELS):
        sg = sgn_ref[l * C:(l + 1) * C, :]
        h = C >> (l + 1)
        if h >= GLA_MIN_BCAST:
            b_ref = jnp.concatenate(
                [jnp.broadcast_to(b[r0 + h:r0 + h + 1, :], (2 * h, DK)) for r0 in range(0, C, 2 * h)], axis=0)
        else:
            fine = l - (GLA_LEVELS - n_fine)
            b_ref = bs[(fine + 1) * C:(fine + 2) * C]
        t = (jnp.where(sg > 0.0, q, k) * jnp.exp((b - b_ref) * sg)).astype(BF16)
        scores = jnp.where(level == l, _dot_nt(t, t), scores)
    intra = _dot(scores.astype(BF16), v)
    q_dec = (q * jnp.exp(b)).astype(BF16)
    k_dec = (k * jnp.exp(b_last - b)).astype(BF16)
    update = _dot_tn(k_dec, v)
    decay_col = jnp.exp(jnp.transpose(jnp.broadcast_to(b_last, (LANE, DK))))
    return intra, q_dec, update, decay_col


def _gla_kernel(q_ref, k_ref, v_ref, r_ref, g1_ref, w2_ref, bg_ref, hg_ref, ms_ref, sgn_ref, lv_ref,
                o_ref, state_ref, *, rows):
    @pl.when(pl.program_id(2) == 0)
    def _():
        state_ref[...] = jnp.zeros_like(state_ref)

    w2 = w2_ref[...]
    b_gate = bg_ref[...]
    head_g = hg_ref[...]
    mstack = ms_ref[...]
    level = lv_ref[...]
    chunks = [pl.ds(c * GLA_CHUNK, GLA_CHUNK) for c in range(rows // GLA_CHUNK)]
    decays = [_gla_decays(g1_ref[sl, :], w2, b_gate, mstack) for sl in chunks]
    local = []
    for sl, bs in zip(chunks, decays):
        q = q_ref[sl, :].astype(F32) * (GLA_DK ** -0.5)
        local.append(_gla_local(q, k_ref[sl, :].astype(F32), v_ref[sl, :], bs, sgn_ref, level))
    state = state_ref[...]
    for sl, (intra, q_dec, update, decay_col) in zip(chunks, local):
        o = _dot(q_dec, state.astype(BF16)) + intra
        state = jnp.concatenate([decay_col] * (state.shape[1] // LANE), axis=1) * state + update
        r = r_ref[sl, :].astype(F32)
        o_ref[sl, :] = (_rmsnorm(o, head_g) * (r * _sigmoid(r))).astype(o_ref.dtype)
    state_ref[...] = state


def _gla(proj, g1, w2, b_gate, head_g, batch, seq, *, rows=1024):
    T = proj.shape[0]
    H, DK, DV = GLA_HEADS, GLA_DK, GLA_DV
    rows = min(rows, seq)
    nc = seq // rows
    rowmap = lambda b, h, c: b * nc + c
    const = lambda b, h, c: (0, 0)
    v_off = 2 * H * DK // DV
    r_off = v_off + H
    mstack, sgn, level = _gla_tables()
    return pl.pallas_call(
        functools.partial(_gla_kernel, rows=rows),
        grid=(batch, H, nc),
        in_specs=[
            pl.BlockSpec((rows, DK), lambda b, h, c: (rowmap(b, h, c), h)),
            pl.BlockSpec((rows, DK), lambda b, h, c: (rowmap(b, h, c), H + h)),
            pl.BlockSpec((rows, DV), lambda b, h, c: (rowmap(b, h, c), v_off + h)),
            pl.BlockSpec((rows, DV), lambda b, h, c: (rowmap(b, h, c), r_off + h)),
            pl.BlockSpec((rows, LANE), lambda b, h, c: (rowmap(b, h, c), 0)),
            pl.BlockSpec((LANE, DK), lambda b, h, c: (0, h)),
            pl.BlockSpec((1, DK), lambda b, h, c: (0, h)),
            pl.BlockSpec((1, DV), lambda b, h, c: (0, h)),
            pl.BlockSpec(mstack.shape, const),
            pl.BlockSpec(sgn.shape, const),
            pl.BlockSpec(level.shape, const),
        ],
        out_specs=pl.BlockSpec((rows, DV), lambda b, h, c: (rowmap(b, h, c), h)),
        out_shape=jax.ShapeDtypeStruct((T, H * DV), BF16),
        scratch_shapes=[pltpu.VMEM((DK, DV), F32)],
        compiler_params=_params("parallel", "parallel", "arbitrary"),
    )(proj, proj, proj, proj, g1, w2, b_gate, head_g, mstack, sgn, level)


def _ffn_kernel(x_ref, xh_ref, g_ref, sh_ref, sc_ref, gate_ref, wv_ref, wg_ref, cwv_ref, cwg_ref,
                cbv_ref, cbg_ref, wd_ref, fg_ref, o_ref, hn_ref, acc_ref, *, tm, per_seq, final_norm):
    i = pl.program_id(0)
    j = pl.program_id(1)
    halo = BF16_ROWS
    sub = SUBLANES
    span = tm // sub
    d_model = x_ref.shape[1]

    @pl.when(j == 0)
    def _():
        g, sh, sc = g_ref[...], sh_ref[...], sc_ref[...]
        h = _normmod(x_ref[...], g, sh, sc)
        h = pltpu.einshape("svd->vsd", h.reshape(sub, span, d_model)).reshape(tm, d_model)
        hn_ref[halo:, :] = h.astype(BF16)
        prev = _normmod(xh_ref[...], g, sh, sc)
        hn_ref[:halo, :] = jnp.where(i % per_seq == 0, 0.0, prev).astype(BF16)
        acc_ref[...] = jnp.zeros_like(acc_ref)

    hn = hn_ref[...]
    sub_id = lax.broadcasted_iota(jnp.int32, (sub, wv_ref.shape[1]), 0)

    def conv(u, cw_ref, cb_ref):
        cw = cw_ref[...]
        um = u[halo:]
        tail = u[halo - sub:halo]
        first1 = pltpu.roll(jnp.where(sub_id == sub - 1, tail, um[tm - sub:]), 1, 0)
        first2 = pltpu.roll(
            jnp.where(sub_id == sub - 1, pltpu.roll(tail, 1, 0), um[tm - 2 * sub:tm - sub]), 1, 0)
        back1 = jnp.concatenate([first1, um[:tm - sub]], axis=0)
        back2 = jnp.concatenate([first2, first1, um[:tm - 2 * sub]], axis=0)
        return cb_ref[...] + cw[2:3, :] * um + cw[1:2, :] * back1 + cw[0:1, :] * back2

    val = conv(_dot(hn, wv_ref[...]), cwv_ref, cbv_ref)
    gt = conv(_dot(hn, wg_ref[...]), cwg_ref, cbg_ref)
    act = (gt * _sigmoid(gt) * val).astype(BF16)
    acc_ref[...] += _dot(act, wd_ref[...])

    @pl.when(j == pl.num_programs(1) - 1)
    def _():
        acc = pltpu.einshape("vsd->svd", acc_ref[...].reshape(span, sub, d_model)).reshape(tm, d_model)
        y = x_ref[...] + (1.0 + gate_ref[...]) * acc
        if final_norm:
            y = _rmsnorm(y, fg_ref[...])
        o_ref[...] = y


def _conv_ffn(x, g, shift, scale, gate, w_up, conv_w, conv_b, w_down, final_g, seq, layer, *,
              final_norm, tm=512, tf=512):
    assert conv_w.shape[1] == CONV_W == 3
    T, D = x.shape
    F = w_down.shape[1]
    tm = min(tm, seq)
    per_seq = seq // tm
    nf = F // tf
    halo = BF16_ROWS
    hb = tm // halo
    batch_map = lambda i, j: (i // per_seq, 0, 0)
    return pl.pallas_call(
        functools.partial(_ffn_kernel, tm=tm, per_seq=per_seq, final_norm=final_norm),
        grid=(T // tm, nf),
        in_specs=[
            pl.BlockSpec((tm, D), lambda i, j: (i, 0)),
            pl.BlockSpec((halo, D), lambda i, j: (jnp.maximum(i * hb - 1, 0), 0)),
            pl.BlockSpec((1, D), lambda i, j: (0, 0)),
            pl.BlockSpec((None, 1, D), batch_map),
            pl.BlockSpec((None, 1, D), batch_map),
            pl.BlockSpec((None, 1, D), batch_map),
            pl.BlockSpec((None, D, tf), lambda i, j: (layer, 0, j)),
            pl.BlockSpec((None, D, tf), lambda i, j: (layer, 0, nf + j)),
            pl.BlockSpec((None, CONV_W, tf), lambda i, j: (layer, 0, j)),
            pl.BlockSpec((None, CONV_W, tf), lambda i, j: (layer, 0, nf + j)),
            pl.BlockSpec((None, 1, tf), lambda i, j: (layer, 0, j)),
            pl.BlockSpec((None, 1, tf), lambda i, j: (layer, 0, nf + j)),
            pl.BlockSpec((None, tf, D), lambda i, j: (layer, j, 0)),
            pl.BlockSpec((1, D), lambda i, j: (0, 0)),
        ],
        out_specs=pl.BlockSpec((tm, D), lambda i, j: (i, 0)),
        out_shape=jax.ShapeDtypeStruct((T, D), F32),
        scratch_shapes=[pltpu.VMEM((tm + halo, D), BF16), pltpu.VMEM((tm, D), F32)],
        compiler_params=_params("parallel", "arbitrary"),
    )(x, x, g, shift, scale, gate, w_up, w_up, conv_w, conv_w, conv_b, conv_b, w_down, final_g)


def _mla_q_kernel(x_ref, g_ref, sh_ref, sc_ref, wdq_ref, qg_ref, wn_ref, wr_ref, wrs_ref,
                  cos_ref, sin_ref, qn_ref, qr_ref, *, scale):
    h = _normmod(x_ref[...], g_ref[...], sh_ref[...], sc_ref[...]).astype(BF16)
    cq = _rmsnorm(_dot(h, wdq_ref[...]), qg_ref[...]).astype(BF16)
    qn_ref[...] = (_dot(cq, wn_ref[...]) * scale).astype(BF16)
    n_rep = wr_ref.shape[1] // LANE
    cos = jnp.concatenate([cos_ref[...]] * n_rep, axis=1)
    sin = jnp.concatenate([sin_ref[...]] * n_rep, axis=1)
    qr = _dot(cq, wr_ref[...]) * cos + _dot(cq, wrs_ref[...]) * sin
    qr_ref[...] = (qr * scale).astype(BF16)


def _mla_q(x, g, shift, scale_mod, w_dq, q_g, w_n, w_r, w_rs, cos, sin, seq, *, tm=512):
    T, D = x.shape
    R = w_dq.shape[1]
    N = w_n.shape[1]
    NR = w_r.shape[1]
    tm = min(tm, seq)
    per_seq = seq // tm
    batch_map = lambda i: (i // per_seq, 0, 0)
    const = lambda i: (0, 0)
    rowblk = lambda i: (i, 0)
    return pl.pallas_call(
        functools.partial(_mla_q_kernel, scale=LOG2E * (MLA_NOPE + MLA_ROPE) ** -0.5),
        grid=(T // tm,),
        in_specs=[
            pl.BlockSpec((tm, D), rowblk),
            pl.BlockSpec((1, D), const),
            pl.BlockSpec((None, 1, D), batch_map),
            pl.BlockSpec((None, 1, D), batch_map),
            pl.BlockSpec((D, R), const),
            pl.BlockSpec((1, R), const),
            pl.BlockSpec((R, N), const),
            pl.BlockSpec((R, NR), const),
            pl.BlockSpec((R, NR), const),
            pl.BlockSpec((tm, LANE), rowblk),
            pl.BlockSpec((tm, LANE), rowblk),
        ],
        out_specs=[pl.BlockSpec((tm, N), rowblk), pl.BlockSpec((tm, NR), rowblk)],
        out_shape=[jax.ShapeDtypeStruct((T, N), BF16), jax.ShapeDtypeStruct((T, NR), BF16)],
        compiler_params=_params("parallel"),
    )(x, g, shift, scale_mod, w_dq, q_g, w_n, w_r, w_rs, cos, sin)


def _mla_kv_kernel(x_ref, g_ref, sh_ref, sc_ref, wdkv_ref, lg_ref, wk_ref, wv_ref,
                   cos_ref, sin_ref, kn_ref, kr_ref, v_ref):
    h = _normmod(x_ref[...], g_ref[...], sh_ref[...], sc_ref[...]).astype(BF16)
    ckv = _dot(h, wdkv_ref[...])
    lat = _rmsnorm(ckv[:, :MLA_KV_RANK], lg_ref[...]).astype(BF16)
    kn_ref[...] = _dot(lat, wk_ref[...]).astype(BF16)
    v_ref[...] = _dot(lat, wv_ref[...]).astype(BF16)
    kr = (ckv[:, MLA_KV_RANK:MLA_KV_RANK + LANE] * cos_ref[...]
          + ckv[:, MLA_KV_RANK + LANE:] * sin_ref[...])
    kr_ref[...] = jnp.concatenate([kr, pltpu.roll(kr, LANE // 2, 1)], axis=1).astype(BF16)


def _mla_kv(x, g, shift, scale_mod, w_dkv, lat_g, w_k, w_v, cos, sin, seq, *, tm=512):
    T, D = x.shape
    R = w_k.shape[0]
    N = w_k.shape[1]
    tm = min(tm, seq)
    per_seq = seq // tm
    batch_map = lambda i: (i // per_seq, 0, 0)
    const = lambda i: (0, 0)
    rowblk = lambda i: (i, 0)
    return pl.pallas_call(
        _mla_kv_kernel,
        grid=(T // tm,),
        in_specs=[
            pl.BlockSpec((tm, D), rowblk),
            pl.BlockSpec((1, D), const),
            pl.BlockSpec((None, 1, D), batch_map),
            pl.BlockSpec((None, 1, D), batch_map),
            pl.BlockSpec((D, w_dkv.shape[1]), const),
            pl.BlockSpec((1, R), const),
            pl.BlockSpec((R, N), const),
            pl.BlockSpec((R, N), const),
            pl.BlockSpec((tm, LANE), rowblk),
            pl.BlockSpec((tm, LANE), rowblk),
        ],
        out_specs=[pl.BlockSpec((tm, N), rowblk), pl.BlockSpec((tm, 2 * LANE), rowblk),
                   pl.BlockSpec((tm, N), rowblk)],
        out_shape=[jax.ShapeDtypeStruct((T, N), BF16), jax.ShapeDtypeStruct((T, 2 * LANE), BF16),
                   jax.ShapeDtypeStruct((T, N), BF16)],
        compiler_params=_params("parallel"),
    )(x, g, shift, scale_mod, w_dkv, lat_g, w_k, w_v, cos, sin)


def _attn_kernel(qn_ref, qr_ref, kn_ref, kr_ref, v_ref, o_ref, *, blk):
    seq = qn_ref.shape[0]
    ones = jnp.ones((blk, LANE), BF16)
    row = lax.broadcasted_iota(jnp.int32, (blk, blk), 0)
    col = lax.broadcasted_iota(jnp.int32, (blk, blk), 1)
    for qi in range(seq // blk):
        qrows = pl.ds(qi * blk, blk)
        q = jnp.concatenate([qn_ref[qrows, :], qr_ref[qrows, :]], axis=1)
        m = jnp.full((blk, LANE), -jnp.inf, F32)
        acc = jnp.zeros((blk, 2 * LANE), F32)
        for ki in range(qi + 1):
            krows = pl.ds(ki * blk, blk)
            k = jnp.concatenate([kn_ref[krows, :], kr_ref[krows, :]], axis=1)
            s = _dot_nt(q, k)
            if ki == qi:
                s = jnp.where(col <= row, s, -jnp.inf)
            m_new = jnp.maximum(m, jnp.max(s, axis=-1, keepdims=True))
            alpha = jnp.exp2(m - m_new)
            p = jnp.exp2(s - jnp.concatenate([m_new] * (blk // LANE), axis=1))
            v1 = jnp.concatenate([v_ref[krows, :], ones], axis=1)
            acc = jnp.concatenate([alpha, alpha], axis=1) * acc + _dot(p.astype(BF16), v1)
            m = m_new
        o_ref[qrows, :] = (acc[:, :LANE] / acc[:, LANE:]).astype(o_ref.dtype)


def _attention(qn, qr, kn, kr, v, batch, seq, *, blk=256):
    T = qn.shape[0]
    H = MLA_HEADS
    blk = min(blk, seq)
    head = lambda b, h: (b, h)
    return pl.pallas_call(
        functools.partial(_attn_kernel, blk=blk),
        grid=(batch, H),
        in_specs=[
            pl.BlockSpec((seq, LANE), head),
            pl.BlockSpec((seq, LANE), lambda b, h: (b, h // 2)),
            pl.BlockSpec((seq, LANE), head),
            pl.BlockSpec((seq, LANE), lambda b, h: (b, h % 2)),
            pl.BlockSpec((seq, LANE), head),
        ],
        out_specs=pl.BlockSpec((seq, LANE), head),
        out_shape=jax.ShapeDtypeStruct((T, H * MLA_V), BF16),
        compiler_params=_params("parallel", "parallel"),
    )(qn, qr, kn, kr, v)


def _pad_cols(w, n):
    return jnp.pad(w, [(0, 0)] * (w.ndim - 1) + [(0, n - w.shape[-1])])


def _rotate_half_cols(w):
    half = w.shape[-1] // 2
    return jnp.concatenate([-w[..., half:], w[..., :half]], axis=-1)


def kernel(x, c, positions, mod_w, mod_b, norm_mix_g, norm_ffn_g, gla_w_in, gla_w_gate2, gla_b_gate,
           gla_head_g, gla_w_o, kv_norm_g, kv_mod_w, kv_mod_b, mla_w_dkv, mla_kv_norm_g, mla_w_ukv,
           mla_w_dq, mla_q_norm_g, mla_w_uq, mla_w_o, ffn_w_up, ffn_conv_w, ffn_conv_b, ffn_w_down,
           final_norm_g):
    B, S, D = x.shape
    T = B * S
    depth = mod_w.shape[0]
    n_gla = gla_w_in.shape[0]
    H = MLA_HEADS

    mod = _mod_linear(c, mod_w, mod_b[:, None, :]).reshape(depth, B, 6, 1, D)
    kv_mod = _mod_linear(c, kv_mod_w[None], kv_mod_b[None, None, :]).reshape(B, 2, 1, D)

    inv = ROPE_THETA ** (-jnp.arange(0, MLA_ROPE, 2, dtype=F32) / MLA_ROPE)
    ang = positions.astype(F32)[..., None] * inv
    cos = jnp.tile(jnp.cos(ang), (1, 1, LANE // (MLA_ROPE // 2))).reshape(T, LANE)
    sin = jnp.tile(jnp.sin(ang), (1, 1, LANE // (MLA_ROPE // 2))).reshape(T, LANE)

    ffn_w_up_bf = ffn_w_up.astype(BF16)
    ffn_w_down_bf = ffn_w_down.astype(BF16)

    xs = x.reshape(T, D)
    for i in range(depth):
        sh_m, sc_m, g_m, sh_f, sc_f, g_f = (mod[i, :, t] for t in range(6))
        g_mix = norm_mix_g[i][None, :]
        if i < n_gla:
            gla_cols = 2 * GLA_HEADS * GLA_DK + 2 * GLA_HEADS * GLA_DV
            w_in = gla_w_in[i][:, :gla_cols].astype(BF16)
            w_g1 = _pad_cols(gla_w_in[i][:, gla_cols:], LANE).astype(BF16)
            proj, g1 = _normmod_matmul(xs, g_mix, sh_m, sc_m, w_in, w_g1, S, tm=1024, tn=1536, out_dtype=BF16)
            w2 = jnp.pad(gla_w_gate2[i], ((0, LANE - GLA_GATE_RANK), (0, 0))).astype(BF16)
            y = _gla(proj, g1, w2, gla_b_gate[i][None, :], gla_head_g[i].reshape(1, -1), B, S)
            w_o = gla_w_o[i].astype(BF16)
        else:
            j = i - n_gla
            if j == 0:
                w_dkv = mla_w_dkv
                w_kr = w_dkv[:, MLA_KV_RANK:]
                w_dkv_cat = jnp.concatenate(
                    [w_dkv[:, :MLA_KV_RANK], _pad_cols(w_kr, LANE), _pad_cols(_rotate_half_cols(w_kr), LANE)],
                    axis=1).astype(BF16)
                w_ukv = mla_w_ukv.reshape(MLA_KV_RANK, H, MLA_NOPE + MLA_V)
                w_k = w_ukv[:, :, :MLA_NOPE].reshape(MLA_KV_RANK, H * MLA_NOPE).astype(BF16)
                w_v = w_ukv[:, :, MLA_NOPE:].reshape(MLA_KV_RANK, H * MLA_V).astype(BF16)
                kn, kr, v = _mla_kv(xs, kv_norm_g[None, :], kv_mod[:, 0], kv_mod[:, 1], w_dkv_cat,
                                    mla_kv_norm_g[None, :], w_k, w_v, cos, sin, S)
            w_uq = mla_w_uq[j].reshape(-1, H, MLA_NOPE + MLA_ROPE)
            rq = w_uq.shape[0]
            w_n = w_uq[:, :, :MLA_NOPE].reshape(rq, H * LANE).astype(BF16)
            w_rope = w_uq[:, :, MLA_NOPE:]
            w_r = w_rope.reshape(rq, H * MLA_ROPE).astype(BF16)
            w_rs = _rotate_half_cols(w_rope).reshape(rq, H * MLA_ROPE).astype(BF16)
            qn, qr = _mla_q(xs, g_mix, sh_m, sc_m, mla_w_dq[j].astype(BF16), mla_q_norm_g[j][None, :],
                            w_n, w_r, w_rs, cos, sin, S)
            y = _attention(qn, qr, kn, kr, v, B, S)
            w_o = mla_w_o[j].astype(BF16)
        xs = _matmul_residual(y, w_o, xs, g_m, S)
        xs = _conv_ffn(xs, norm_ffn_g[i][None, :], sh_f, sc_f, g_f, ffn_w_up_bf, ffn_conv_w,
                       ffn_conv_b[:, None, :], ffn_w_down_bf, final_norm_g[None, :], S, i,
                       final_norm=(i == depth - 1))
    return xs.reshape(B, S, D)
```

```python
import functools

import jax
import jax.numpy as jnp
import numpy as np
from jax import lax
from jax.experimental import pallas as pl
from jax.experimental.pallas import tpu as pltpu

F32 = jnp.float32
BF16 = jnp.bfloat16

EPS = 1e-6
LOG2E = 1.4426950408889634
LANE = 128
SUBLANES = 8
VMEM_LIMIT_BYTES = 56 << 20

GLA_HEADS = 4
GLA_DK = 256
GLA_DV = 512
GLA_GATE_RANK = 16
GLA_GATE_NORM = 16.0
GLA_CHUNK = 64
GLA_LEVELS = 6
GLA_MIN_BCAST = 4

MLA_HEADS = 16
MLA_NOPE = 128
MLA_ROPE = 64
MLA_V = 128
MLA_KV_RANK = 512
ROPE_THETA = 10000.0

CONV_W = 3


def _params(*sem):
    return pltpu.CompilerParams(dimension_semantics=sem, vmem_limit_bytes=VMEM_LIMIT_BYTES)


def _dot(a, b):
    return jnp.dot(a, b, preferred_element_type=F32)


def _dot_nt(a, b):
    return lax.dot_general(a, b, (((1,), (1,)), ((), ())), preferred_element_type=F32)


def _dot_tn(a, b):
    return lax.dot_general(a, b, (((0,), (0,)), ((), ())), preferred_element_type=F32)


def _sigmoid(x):
    return 1.0 / (1.0 + jnp.exp(-x))


def _rmsnorm(x, g):
    return x * lax.rsqrt(jnp.mean(x * x, axis=-1, keepdims=True) + EPS) * g


def _normmod(x, g, shift, scale):
    return _rmsnorm(x, g) * (1.0 + scale) + shift


def _mod_kernel(c_ref, w_ref, b_ref, o_ref):
    c = c_ref[...]
    c_act = (c * _sigmoid(c)).astype(BF16)
    o_ref[...] = _dot(c_act, w_ref[...].astype(BF16)) + b_ref[...]


def _mod_linear(c, w, b, *, tn=1024):
    L, D, N = w.shape
    B = c.shape[0]
    return pl.pallas_call(
        _mod_kernel,
        grid=(L, N // tn),
        in_specs=[
            pl.BlockSpec((B, D), lambda l, j: (0, 0)),
            pl.BlockSpec((None, D, tn), lambda l, j: (l, 0, j)),
            pl.BlockSpec((None, 1, tn), lambda l, j: (l, 0, j)),
        ],
        out_specs=pl.BlockSpec((None, B, tn), lambda l, j: (l, 0, j)),
        out_shape=jax.ShapeDtypeStruct((L, B, N), F32),
        compiler_params=_params("parallel", "parallel"),
    )(c, w, b)


def _normmod_matmul_kernel(x_ref, g_ref, sh_ref, sc_ref, w_ref, ws_ref, o_ref, os_ref, hn_ref):
    @pl.when(pl.program_id(1) == 0)
    def _():
        hn_ref[...] = _normmod(x_ref[...], g_ref[...], sh_ref[...], sc_ref[...]).astype(BF16)
        os_ref[...] = _dot(hn_ref[...], ws_ref[...]).astype(os_ref.dtype)

    o_ref[...] = _dot(hn_ref[...], w_ref[...]).astype(o_ref.dtype)


def _normmod_matmul(x, g, shift, scale, w, w_side, seq, *, tm, tn, out_dtype):
    T, D = x.shape
    N = w.shape[1]
    tm = min(tm, seq)
    per_seq = seq // tm
    return pl.pallas_call(
        _normmod_matmul_kernel,
        grid=(T // tm, N // tn),
        in_specs=[
            pl.BlockSpec((tm, D), lambda i, j: (i, 0)),
            pl.BlockSpec((1, D), lambda i, j: (0, 0)),
            pl.BlockSpec((None, 1, D), lambda i, j: (i // per_seq, 0, 0)),
            pl.BlockSpec((None, 1, D), lambda i, j: (i // per_seq, 0, 0)),
            pl.BlockSpec((D, tn), lambda i, j: (0, j)),
            pl.BlockSpec((D, LANE), lambda i, j: (0, 0)),
        ],
        out_specs=[pl.BlockSpec((tm, tn), lambda i, j: (i, j)),
                   pl.BlockSpec((tm, LANE), lambda i, j: (i, 0))],
        out_shape=[jax.ShapeDtypeStruct((T, N), out_dtype), jax.ShapeDtypeStruct((T, LANE), out_dtype)],
        scratch_shapes=[pltpu.VMEM((tm, D), BF16)],
        compiler_params=_params("parallel", "arbitrary"),
    )(x, g, shift, scale, w, w_side)


def _matmul_residual_kernel(a_ref, w_ref, res_ref, gate_ref, o_ref):
    o_ref[...] = res_ref[...] + (1.0 + gate_ref[...]) * _dot(a_ref[...], w_ref[...])


def _matmul_residual(a, w, res, gate, seq, *, tm=512):
    T, K = a.shape
    N = w.shape[1]
    tn = N
    tm = min(tm, seq)
    per_seq = seq // tm
    return pl.pallas_call(
        _matmul_residual_kernel,
        grid=(T // tm, N // tn),
        in_specs=[
            pl.BlockSpec((tm, K), lambda i, j: (i, 0)),
            pl.BlockSpec((K, tn), lambda i, j: (0, j)),
            pl.BlockSpec((tm, tn), lambda i, j: (i, j)),
            pl.BlockSpec((None, 1, tn), lambda i, j: (i // per_seq, 0, j)),
        ],
        out_specs=pl.BlockSpec((tm, tn), lambda i, j: (i, j)),
        out_shape=jax.ShapeDtypeStruct((T, N), F32),
        compiler_params=_params("parallel", "parallel"),
    )(a, w, res, gate)


def _gla_tables():
    C = GLA_CHUNK
    r = np.arange(C)
    tri = (r[None, :] <= r[:, None]).astype(np.float32)
    mats, sgn = [tri], []
    level = np.full((C, C), -1, np.int32)
    level[r, r] = GLA_LEVELS
    for l in range(GLA_LEVELS):
        h = C >> (l + 1)
        if h < GLA_MIN_BCAST:
            mats.append(tri[(r // (2 * h)) * (2 * h) + h])
        is_q = (r // h) % 2 == 1
        sgn.append(np.where(is_q, 1.0, -1.0))
        level[is_q[:, None] & ((r[None, :] // h) == (r[:, None] // h) - 1)] = l
    mstack = jnp.asarray(np.concatenate(mats, axis=0), BF16)
    sgn = jnp.asarray(np.repeat(np.concatenate(sgn)[:, None], GLA_DK, axis=1), F32)
    return mstack, sgn, jnp.asarray(level)


def _gla_decays(g1, w2, b_gate, mstack):
    z = _dot(g1, w2) + b_gate
    log_a = (jnp.minimum(z, 0.0) - jnp.log(1.0 + jnp.exp(-jnp.abs(z)))) * (1.0 / GLA_GATE_NORM)
    la_hi = log_a.astype(BF16)
    la_lo = (log_a - la_hi.astype(F32)).astype(BF16)
    return _dot(mstack, la_hi) + _dot(mstack, la_lo)


def _gla_local(q, k, v, bs, sgn_ref, level):
    C, DK = q.shape
    b = bs[:C]
    b_last = b[C - 1:C, :]
    n_fine = bs.shape[0] // C - 1
    scores = jnp.where(level == GLA_LEVELS, jnp.sum(q * k, axis=-1, keepdims=True), 0.0)
    for l in range(GLA_LEVELS):
        sg = sgn_ref[l * C:(l + 1) * C, :]
        h = C >> (l + 1)
        if h >= GLA_MIN_BCAST:
            b_ref = jnp.concatenate(
                [jnp.broadcast_to(b[r0 + h:r0 + h + 1, :], (2 * h, DK)) for r0 in range(0, C, 2 * h)], axis=0)
        else:
            fine = l - (GLA_LEVELS - n_fine)
            b_ref = bs[(fine + 1) * C:(fine + 2) * C]
        t = (jnp.where(sg > 0.0, q, k) * jnp.exp((b - b_ref) * sg)).astype(BF16)
        scores = jnp.where(level == l, _dot_nt(t, t), scores)
    intra = _dot(scores.astype(BF16), v)
    q_dec = (q * jnp.exp(b)).astype(BF16)
    k_dec = (k * jnp.exp(b_last - b)).astype(BF16)
    update = _dot_tn(k_dec, v)
    decay_col = jnp.exp(jnp.transpose(jnp.broadcast_to(b_last, (LANE, DK))))
    return intra, q_dec, update, decay_col


def _gla_kernel(q_ref, k_ref, v_ref, r_ref, g1_ref, w2_ref, bg_ref, hg_ref, ms_ref, sgn_ref, lv_ref,
                o_ref, state_ref, *, rows):
    @pl.when(pl.program_id(2) == 0)
    def _():
        state_ref[...] = jnp.zeros_like(state_ref)

    w2 = w2_ref[...]
    b_gate = bg_ref[...]
    head_g = hg_ref[...]
    mstack = ms_ref[...]
    level = lv_ref[...]
    chunks = [pl.ds(c * GLA_CHUNK, GLA_CHUNK) for c in range(rows // GLA_CHUNK)]
    decays = [_gla_decays(g1_ref[sl, :], w2, b_gate, mstack) for sl in chunks]
    local = []
    for sl, bs in zip(chunks, decays):
        q = q_ref[sl, :].astype(F32) * (GLA_DK ** -0.5)
        local.append(_gla_local(q, k_ref[sl, :].astype(F32), v_ref[sl, :], bs, sgn_ref, level))
    state = state_ref[...]
    for sl, (intra, q_dec, update, decay_col) in zip(chunks, local):
        o = _dot(q_dec, state.astype(BF16)) + intra
        state = jnp.concatenate([decay_col] * (state.shape[1] // LANE), axis=1) * state + update
        r = r_ref[sl, :].astype(F32)
        o_ref[sl, :] = (_rmsnorm(o, head_g) * (r * _sigmoid(r))).astype(o_ref.dtype)
    state_ref[...] = state


def _gla(proj, g1, w2, b_gate, head_g, batch, seq, *, rows=1024):
    T = proj.shape[0]
    H, DK, DV = GLA_HEADS, GLA_DK, GLA_DV
    rows = min(rows, seq)
    nc = seq // rows
    rowmap = lambda b, h, c: b * nc + c
    const = lambda b, h, c: (0, 0)
    v_off = 2 * H * DK // DV
    r_off = v_off + H
    mstack, sgn, level = _gla_tables()
    return pl.pallas_call(
        functools.partial(_gla_kernel, rows=rows),
        grid=(batch, H, nc),
        in_specs=[
            pl.BlockSpec((rows, DK), lambda b, h, c: (rowmap(b, h, c), h)),
            pl.BlockSpec((rows, DK), lambda b, h, c: (rowmap(b, h, c), H + h)),
            pl.BlockSpec((rows, DV), lambda b, h, c: (rowmap(b, h, c), v_off + h)),
            pl.BlockSpec((rows, DV), lambda b, h, c: (rowmap(b, h, c), r_off + h)),
            pl.BlockSpec((rows, LANE), lambda b, h, c: (rowmap(b, h, c), 0)),
            pl.BlockSpec((LANE, DK), lambda b, h, c: (0, h)),
            pl.BlockSpec((1, DK), lambda b, h, c: (0, h)),
            pl.BlockSpec((1, DV), lambda b, h, c: (0, h)),
            pl.BlockSpec(mstack.shape, const),
            pl.BlockSpec(sgn.shape, const),
            pl.BlockSpec(level.shape, const),
        ],
        out_specs=pl.BlockSpec((rows, DV), lambda b, h, c: (rowmap(b, h, c), h)),
        out_shape=jax.ShapeDtypeStruct((T, H * DV), BF16),
        scratch_shapes=[pltpu.VMEM((DK, DV), F32)],
        compiler_params=_params("parallel", "parallel", "arbitrary"),
    )(proj, proj, proj, proj, g1, w2, b_gate, head_g, mstack, sgn, level)


def _ffn_kernel(x_ref, g_ref, sh_ref, sc_ref, gate_ref, wv_ref, wg_ref, cwv_ref, cwg_ref,
                cbv_ref, cbg_ref, wd_ref, fg_ref, o_ref, hn_ref, acc_ref, tail_ref, *, tm, per_seq,
                final_norm):
    i = pl.program_id(0)
    j = pl.program_id(1)
    sub = SUBLANES
    span = tm // sub
    d_model = x_ref.shape[1]

    @pl.when(jnp.logical_and(i == 0, j == 0))
    def _():
        tail_ref[...] = jnp.zeros_like(tail_ref)

    @pl.when(j == 0)
    def _():
        h = _normmod(x_ref[...], g_ref[...], sh_ref[...], sc_ref[...])
        h = pltpu.einshape("svd->vsd", h.reshape(sub, span, d_model)).reshape(tm, d_model)
        hn_ref[...] = h.astype(BF16)
        acc_ref[...] = jnp.zeros_like(acc_ref)

    hn = hn_ref[...]
    sub_id = lax.broadcasted_iota(jnp.int32, (sub, wv_ref.shape[1]), 0)
    seq_start = i % per_seq == 0

    def conv(um, k, cw_ref, cb_ref):
        cw = cw_ref[...]
        tail = jnp.where(seq_start, 0.0, tail_ref[j, k])
        tail_ref[j, k] = jnp.where(sub_id == sub - 1, um[tm - sub:], pltpu.roll(um[tm - 2 * sub:tm - sub], sub - 1, 0))
        first1 = pltpu.roll(jnp.where(sub_id == sub - 1, tail, um[tm - sub:]), 1, 0)
        first2 = pltpu.roll(
            jnp.where(sub_id == sub - 1, pltpu.roll(tail, 1, 0), um[tm - 2 * sub:tm - sub]), 1, 0)
        back1 = jnp.concatenate([first1, um[:tm - sub]], axis=0)
        back2 = jnp.concatenate([first2, first1, um[:tm - 2 * sub]], axis=0)
        return cb_ref[...] + cw[2:3, :] * um + cw[1:2, :] * back1 + cw[0:1, :] * back2

    val = conv(_dot(hn, wv_ref[...]), 0, cwv_ref, cbv_ref)
    gt = conv(_dot(hn, wg_ref[...]), 1, cwg_ref, cbg_ref)
    act = (gt * _sigmoid(gt) * val).astype(BF16)
    acc_ref[...] += _dot(act, wd_ref[...])

    @pl.when(j == pl.num_programs(1) - 1)
    def _():
        acc = pltpu.einshape("vsd->svd", acc_ref[...].reshape(span, sub, d_model)).reshape(tm, d_model)
        y = x_ref[...] + (1.0 + gate_ref[...]) * acc
        if final_norm:
            y = _rmsnorm(y, fg_ref[...])
        o_ref[...] = y


def _conv_ffn(x, g, shift, scale, gate, w_up, conv_w, conv_b, w_down, final_g, seq, layer, *,
              final_norm, tm=512, tf=512):
    assert conv_w.shape[1] == CONV_W == 3
    T, D = x.shape
    F = w_down.shape[1]
    tm = min(tm, seq)
    per_seq = seq // tm
    nf = F // tf
    batch_map = lambda i, j: (i // per_seq, 0, 0)
    return pl.pallas_call(
        functools.partial(_ffn_kernel, tm=tm, per_seq=per_seq, final_norm=final_norm),
        grid=(T // tm, nf),
        in_specs=[
            pl.BlockSpec((tm, D), lambda i, j: (i, 0)),
            pl.BlockSpec((1, D), lambda i, j: (0, 0)),
            pl.BlockSpec((None, 1, D), batch_map),
            pl.BlockSpec((None, 1, D), batch_map),
            pl.BlockSpec((None, 1, D), batch_map),
            pl.BlockSpec((None, D, tf), lambda i, j: (layer, 0, j)),
            pl.BlockSpec((None, D, tf), lambda i, j: (layer, 0, nf + j)),
            pl.BlockSpec((None, CONV_W, tf), lambda i, j: (layer, 0, j)),
            pl.BlockSpec((None, CONV_W, tf), lambda i, j: (layer, 0, nf + j)),
            pl.BlockSpec((None, 1, tf), lambda i, j: (layer, 0, j)),
            pl.BlockSpec((None, 1, tf), lambda i, j: (layer, 0, nf + j)),
            pl.BlockSpec((None, tf, D), lambda i, j: (layer, j, 0)),
            pl.BlockSpec((1, D), lambda i, j: (0, 0)),
        ],
        out_specs=pl.BlockSpec((tm, D), lambda i, j: (i, 0)),
        out_shape=jax.ShapeDtypeStruct((T, D), F32),
        scratch_shapes=[pltpu.VMEM((tm, D), BF16), pltpu.VMEM((tm, D), F32),
                        pltpu.VMEM((nf, 2, SUBLANES, tf), F32)],
        compiler_params=_params("arbitrary", "arbitrary"),
    )(x, g, shift, scale, gate, w_up, w_up, conv_w, conv_w, conv_b, conv_b, w_down, final_g)


def _mla_q_kernel(x_ref, g_ref, sh_ref, sc_ref, wdq_ref, qg_ref, wn_ref, wr_ref, wrs_ref,
                  cos_ref, sin_ref, qn_ref, qr_ref, *, scale):
    h = _normmod(x_ref[...], g_ref[...], sh_ref[...], sc_ref[...]).astype(BF16)
    cq = _rmsnorm(_dot(h, wdq_ref[...]), qg_ref[...]).astype(BF16)
    qn_ref[...] = (_dot(cq, wn_ref[...]) * scale).astype(BF16)
    n_rep = wr_ref.shape[1] // LANE
    cos = jnp.concatenate([cos_ref[...]] * n_rep, axis=1)
    sin = jnp.concatenate([sin_ref[...]] * n_rep, axis=1)
    qr = _dot(cq, wr_ref[...]) * cos + _dot(cq, wrs_ref[...]) * sin
    qr_ref[...] = (qr * scale).astype(BF16)


def _mla_q(x, g, shift, scale_mod, w_dq, q_g, w_n, w_r, w_rs, cos, sin, seq, *, tm=512):
    T, D = x.shape
    R = w_dq.shape[1]
    N = w_n.shape[1]
    NR = w_r.shape[1]
    tm = min(tm, seq)
    per_seq = seq // tm
    batch_map = lambda i: (i // per_seq, 0, 0)
    const = lambda i: (0, 0)
    rowblk = lambda i: (i, 0)
    return pl.pallas_call(
        functools.partial(_mla_q_kernel, scale=LOG2E * (MLA_NOPE + MLA_ROPE) ** -0.5),
        grid=(T // tm,),
        in_specs=[
            pl.BlockSpec((tm, D), rowblk),
            pl.BlockSpec((1, D), const),
            pl.BlockSpec((None, 1, D), batch_map),
            pl.BlockSpec((None, 1, D), batch_map),
            pl.BlockSpec((D, R), const),
            pl.BlockSpec((1, R), const),
            pl.BlockSpec((R, N), const),
            pl.BlockSpec((R, NR), const),
            pl.BlockSpec((R, NR), const),
            pl.BlockSpec((tm, LANE), rowblk),
            pl.BlockSpec((tm, LANE), rowblk),
        ],
        out_specs=[pl.BlockSpec((tm, N), rowblk), pl.BlockSpec((tm, NR), rowblk)],
        out_shape=[jax.ShapeDtypeStruct((T, N), BF16), jax.ShapeDtypeStruct((T, NR), BF16)],
        compiler_params=_params("parallel"),
    )(x, g, shift, scale_mod, w_dq, q_g, w_n, w_r, w_rs, cos, sin)


def _mla_kv_kernel(x_ref, g_ref, sh_ref, sc_ref, wdkv_ref, lg_ref, wk_ref, wv_ref,
                   cos_ref, sin_ref, kn_ref, kr_ref, v_ref):
    h = _normmod(x_ref[...], g_ref[...], sh_ref[...], sc_ref[...]).astype(BF16)
    ckv = _dot(h, wdkv_ref[...])
    lat = _rmsnorm(ckv[:, :MLA_KV_RANK], lg_ref[...]).astype(BF16)
    kn_ref[...] = _dot(lat, wk_ref[...]).astype(BF16)
    v_ref[...] = _dot(lat, wv_ref[...]).astype(BF16)
    kr = (ckv[:, MLA_KV_RANK:MLA_KV_RANK + LANE] * cos_ref[...]
          + ckv[:, MLA_KV_RANK + LANE:] * sin_ref[...])
    kr_ref[...] = jnp.concatenate([kr, pltpu.roll(kr, LANE // 2, 1)], axis=1).astype(BF16)


def _mla_kv(x, g, shift, scale_mod, w_dkv, lat_g, w_k, w_v, cos, sin, seq, *, tm=512):
    T, D = x.shape
    R = w_k.shape[0]
    N = w_k.shape[1]
    tm = min(tm, seq)
    per_seq = seq // tm
    batch_map = lambda i: (i // per_seq, 0, 0)
    const = lambda i: (0, 0)
    rowblk = lambda i: (i, 0)
    return pl.pallas_call(
        _mla_kv_kernel,
        grid=(T // tm,),
        in_specs=[
            pl.BlockSpec((tm, D), rowblk),
            pl.BlockSpec((1, D), const),
            pl.BlockSpec((None, 1, D), batch_map),
            pl.BlockSpec((None, 1, D), batch_map),
            pl.BlockSpec((D, w_dkv.shape[1]), const),
            pl.BlockSpec((1, R), const),
            pl.BlockSpec((R, N), const),
            pl.BlockSpec((R, N), const),
            pl.BlockSpec((tm, LANE), rowblk),
            pl.BlockSpec((tm, LANE), rowblk),
        ],
        out_specs=[pl.BlockSpec((tm, N), rowblk), pl.BlockSpec((tm, 2 * LANE), rowblk),
                   pl.BlockSpec((tm, N), rowblk)],
        out_shape=[jax.ShapeDtypeStruct((T, N), BF16), jax.ShapeDtypeStruct((T, 2 * LANE), BF16),
                   jax.ShapeDtypeStruct((T, N), BF16)],
        compiler_params=_params("parallel"),
    )(x, g, shift, scale_mod, w_dkv, lat_g, w_k, w_v, cos, sin)


def _attn_kernel(qn_ref, qr_ref, kn_ref, kr_ref, v_ref, o_ref, *, blk):
    seq = qn_ref.shape[0]
    ones = jnp.ones((blk, LANE), BF16)
    row = lax.broadcasted_iota(jnp.int32, (blk, blk), 0)
    col = lax.broadcasted_iota(jnp.int32, (blk, blk), 1)
    for qi in range(seq // blk):
        qrows = pl.ds(qi * blk, blk)
        q = jnp.concatenate([qn_ref[qrows, :], qr_ref[qrows, :]], axis=1)
        m = jnp.full((blk, LANE), -jnp.inf, F32)
        acc = jnp.zeros((blk, 2 * LANE), F32)
        for ki in range(qi + 1):
            krows = pl.ds(ki * blk, blk)
            k = jnp.concatenate([kn_ref[krows, :], kr_ref[krows, :]], axis=1)
            s = _dot_nt(q, k)
            if ki == qi:
                s = jnp.where(col <= row, s, -jnp.inf)
            m_new = jnp.maximum(m, jnp.max(s, axis=-1, keepdims=True))
            alpha = jnp.exp2(m - m_new)
            p = jnp.exp2(s - jnp.concatenate([m_new] * (blk // LANE), axis=1))
            v1 = jnp.concatenate([v_ref[krows, :], ones], axis=1)
            acc = jnp.concatenate([alpha, alpha], axis=1) * acc + _dot(p.astype(BF16), v1)
            m = m_new
        o_ref[qrows, :] = (acc[:, :LANE] / acc[:, LANE:]).astype(o_ref.dtype)


def _attention(qn, qr, kn, kr, v, batch, seq, *, blk=256):
    T = qn.shape[0]
    H = MLA_HEADS
    blk = min(blk, seq)
    head = lambda b, h: (b, h)
    return pl.pallas_call(
        functools.partial(_attn_kernel, blk=blk),
        grid=(batch, H),
        in_specs=[
            pl.BlockSpec((seq, LANE), head),
            pl.BlockSpec((seq, LANE), lambda b, h: (b, h // 2)),
            pl.BlockSpec((seq, LANE), head),
            pl.BlockSpec((seq, LANE), lambda b, h: (b, h % 2)),
            pl.BlockSpec((seq, LANE), head),
        ],
        out_specs=pl.BlockSpec((seq, LANE), head),
        out_shape=jax.ShapeDtypeStruct((T, H * MLA_V), BF16),
        compiler_params=_params("parallel", "parallel"),
    )(qn, qr, kn, kr, v)


def _pad_cols(w, n):
    return jnp.pad(w, [(0, 0)] * (w.ndim - 1) + [(0, n - w.shape[-1])])


def _rotate_half_cols(w):
    half = w.shape[-1] // 2
    return jnp.concatenate([-w[..., half:], w[..., :half]], axis=-1)


def kernel(x, c, positions, mod_w, mod_b, norm_mix_g, norm_ffn_g, gla_w_in, gla_w_gate2, gla_b_gate,
           gla_head_g, gla_w_o, kv_norm_g, kv_mod_w, kv_mod_b, mla_w_dkv, mla_kv_norm_g, mla_w_ukv,
           mla_w_dq, mla_q_norm_g, mla_w_uq, mla_w_o, ffn_w_up, ffn_conv_w, ffn_conv_b, ffn_w_down,
           final_norm_g):
    B, S, D = x.shape
    T = B * S
    depth = mod_w.shape[0]
    n_gla = gla_w_in.shape[0]
    H = MLA_HEADS

    mod = _mod_linear(c, mod_w, mod_b[:, None, :]).reshape(depth, B, 6, 1, D)
    kv_mod = _mod_linear(c, kv_mod_w[None], kv_mod_b[None, None, :]).reshape(B, 2, 1, D)

    inv = ROPE_THETA ** (-jnp.arange(0, MLA_ROPE, 2, dtype=F32) / MLA_ROPE)
    ang = positions.astype(F32)[..., None] * inv
    cos = jnp.tile(jnp.cos(ang), (1, 1, LANE // (MLA_ROPE // 2))).reshape(T, LANE)
    sin = jnp.tile(jnp.sin(ang), (1, 1, LANE // (MLA_ROPE // 2))).reshape(T, LANE)

    ffn_w_up_bf = ffn_w_up.astype(BF16)
    ffn_w_down_bf = ffn_w_down.astype(BF16)

    xs = x.reshape(T, D)
    for i in range(depth):
        sh_m, sc_m, g_m, sh_f, sc_f, g_f = (mod[i, :, t] for t in range(6))
        g_mix = norm_mix_g[i][None, :]
        if i < n_gla:
            gla_cols = 2 * GLA_HEADS * GLA_DK + 2 * GLA_HEADS * GLA_DV
            w_in = gla_w_in[i][:, :gla_cols].astype(BF16)
            w_g1 = _pad_cols(gla_w_in[i][:, gla_cols:], LANE).astype(BF16)
            proj, g1 = _normmod_matmul(xs, g_mix, sh_m, sc_m, w_in, w_g1, S, tm=1024, tn=1536, out_dtype=BF16)
            w2 = jnp.pad(gla_w_gate2[i], ((0, LANE - GLA_GATE_RANK), (0, 0))).astype(BF16)
            y = _gla(proj, g1, w2, gla_b_gate[i][None, :], gla_head_g[i].reshape(1, -1), B, S)
            w_o = gla_w_o[i].astype(BF16)
        else:
            j = i - n_gla
            if j == 0:
                w_dkv = mla_w_dkv
                w_kr = w_dkv[:, MLA_KV_RANK:]
                w_dkv_cat = jnp.concatenate(
                    [w_dkv[:, :MLA_KV_RANK], _pad_cols(w_kr, LANE), _pad_cols(_rotate_half_cols(w_kr), LANE)],
                    axis=1).astype(BF16)
                w_ukv = mla_w_ukv.reshape(MLA_KV_RANK, H, MLA_NOPE + MLA_V)
                w_k = w_ukv[:, :, :MLA_NOPE].reshape(MLA_KV_RANK, H * MLA_NOPE).astype(BF16)
                w_v = w_ukv[:, :, MLA_NOPE:].reshape(MLA_KV_RANK, H * MLA_V).astype(BF16)
                kn, kr, v = _mla_kv(xs, kv_norm_g[None, :], kv_mod[:, 0], kv_mod[:, 1], w_dkv_cat,
                                    mla_kv_norm_g[None, :], w_k, w_v, cos, sin, S)
            w_uq = mla_w_uq[j].reshape(-1, H, MLA_NOPE + MLA_ROPE)
            rq = w_uq.shape[0]
            w_n = w_uq[:, :, :MLA_NOPE].reshape(rq, H * LANE).astype(BF16)
            w_rope = w_uq[:, :, MLA_NOPE:]
            w_r = w_rope.reshape(rq, H * MLA_ROPE).astype(BF16)
            w_rs = _rotate_half_cols(w_rope).reshape(rq, H * MLA_ROPE).astype(BF16)
            qn, qr = _mla_q(xs, g_mix, sh_m, sc_m, mla_w_dq[j].astype(BF16), mla_q_norm_g[j][None, :],
                            w_n, w_r, w_rs, cos, sin, S)
            y = _attention(qn, qr, kn, kr, v, B, S)
            w_o = mla_w_o[j].astype(BF16)
        xs = _matmul_residual(y, w_o, xs, g_m, S)
        xs = _conv_ffn(xs, norm_ffn_g[i][None, :], sh_f, sc_f, g_f, ffn_w_up_bf, ffn_conv_w,
                       ffn_conv_b[:, None, :], ffn_w_down_bf, final_norm_g[None, :], S, i,
                       final_norm=(i == depth - 1))
    return xs.reshape(B, S, D)
```
